```python
import math
import jax
import jax.numpy as jnp
from jax import lax
import numpy as np

D_MODEL = 2048
BATCH = 4
SEQ = 2048
DEPTH = 2
DEC_BATCH = 128
DEC_SEQ = 8
PAST_LEN = 16384
PAGE_SIZE = 128

MIX_W = D_MODEL
N_BRANCH = 3
SSD_HEAD_DIM = 64
SSD_HEADS = MIX_W // SSD_HEAD_DIM
SSD_GROUPS = 4
SSD_STATE = 128
SSD_CONV = 4
SSD_CONV_DIM = MIX_W + 2 * SSD_GROUPS * SSD_STATE
SSD_CHUNK = 64
ML_HEADS = 4
ML_DV = MIX_W // ML_HEADS
ML_DK = ML_DV // 2
ML_CHUNK = 64
HG_EXPAND = 128
HG_HEADS = MIX_W // HG_EXPAND
HG_DK = HG_EXPAND
HG_DV = MIX_W // HG_HEADS
HG_CHUNK = 64
MEM_LEN = 256
X_HEADS = 4
X_HEAD_DIM = D_MODEL // X_HEADS
D_FF = ((8 * D_MODEL // 3 + 127) // 128) * 128
DN_ALPHA = (2.0 * DEPTH) ** 0.25
DN_BETA = (8.0 * DEPTH) ** -0.25
LN_EPS = 1e-5
RMS_EPS = 1e-6
IN_SPLITS = (MIX_W, SSD_CONV_DIM, SSD_HEADS,
             ML_HEADS * ML_DK, ML_HEADS * ML_DK, MIX_W, MIX_W,
             ML_HEADS, ML_HEADS,
             MIX_W, MIX_W, MIX_W, MIX_W,
             N_BRANCH * D_MODEL)
N_IN = sum(IN_SPLITS)

kernel_name = 'hybrid_ssd_mlstm_hgrn2_step'


def split_cols(h, sizes):
    parts, start = [], 0
    for s in sizes:
        parts.append(h[..., start:start + s])
        start += s
    return parts


def layer_norm(x, g, b):
    xf = x.astype(jnp.float32)
    mu = jnp.mean(xf, -1, keepdims=True)
    var = jnp.mean(jnp.square(xf - mu), -1, keepdims=True)
    return ((xf - mu) * lax.rsqrt(var + LN_EPS) * g.astype(jnp.float32) + b.astype(jnp.float32)).astype(x.dtype)


def rms_norm(x, g):
    xf = x.astype(jnp.float32)
    return xf * lax.rsqrt(jnp.mean(xf * xf, -1, keepdims=True) + RMS_EPS) * g.astype(jnp.float32)


def swiglu(x, w1, w3, w2):
    return (jax.nn.silu(x @ w1) * (x @ w3)) @ w2


def to_chunks(a, cs):
    nb, L = a.shape[0], a.shape[1]
    return jnp.moveaxis(a.reshape(nb, L // cs, cs, *a.shape[2:]), 1, 0)


def from_chunks(a):
    n, nb, cs = a.shape[0], a.shape[1], a.shape[2]
    return jnp.moveaxis(a, 0, 1).reshape(nb, n * cs, *a.shape[3:])


def causal_conv(u, buf, w, b):
    L = u.shape[1]
    up = jnp.concatenate([buf.astype(u.dtype), u], axis=1)
    out = b + up[:, 0:L] * w[0]
    for j in range(1, SSD_CONV):
        out = out + up[:, j:j + L] * w[j]
    return out, up[:, L:]


def ssd_scan(x, dt, a, bm, cm, h0):
    cs = math.gcd(x.shape[1], SSD_CHUNK)
    mask = jnp.tril(jnp.ones((cs, cs), bool))

    def step(h, inp):
        xc, dtc, ac, bc, cc = inp
        acum = jnp.cumsum(ac, axis=1)
        seg = acum[:, :, None] - acum[:, None, :]
        decay = jnp.exp(jnp.where(mask[None, :, :, None, None], seg, -jnp.inf))
        cb = jnp.einsum('btgn,bsgn->btsg', cc, bc)
        w = cb[..., None] * decay * dtc[:, None]
        y = jnp.einsum('btsgh,bsghp->btghp', w, xc)
        y = y + jnp.einsum('btgn,bghpn->btghp', cc, h) * jnp.exp(acum)[..., None]
        tail = jnp.exp(acum[:, -1:] - acum) * dtc
        h = h * jnp.exp(acum[:, -1])[..., None, None] + jnp.einsum('bsgh,bsgn,bsghp->bghpn', tail, bc, xc)
        return h, y

    xs = (to_chunks(x, cs), to_chunks(dt, cs), to_chunks(a, cs), to_chunks(bm, cs), to_chunks(cm, cs))
    h, ys = lax.scan(step, h0, xs)
    return from_chunks(ys), h


def mlstm_scan(q, k, v, logi, logf, c0, n0, m0):
    cs = math.gcd(q.shape[1], ML_CHUNK)
    mask = jnp.tril(jnp.ones((cs, cs), bool))

    def step(carry, inp):
        c, n, m = carry
        qc, kc, vc, ic, fc = inp
        b = jnp.cumsum(fc, axis=1)
        dmat = b[:, :, None] - b[:, None, :] + ic[:, None, :]
        dmat = jnp.where(mask[None, :, :, None], dmat, -jnp.inf)
        prev = b + m[:, None]
        mt = jnp.maximum(prev, jnp.max(dmat, axis=2))
        wts = jnp.exp(dmat - mt[:, :, None])
        sprev = jnp.exp(prev - mt)
        qk = jnp.einsum('bthd,bshd->btsh', qc, kc) * wts
        num = jnp.einsum('btsh,bshv->bthv', qk, vc) + jnp.einsum('bthd,bhdv->bthv', qc, c) * sprev[..., None]
        den = jnp.sum(qk, axis=2) + jnp.einsum('bthd,bhd->bth', qc, n) * sprev
        h = num / jnp.maximum(jnp.abs(den), jnp.exp(-mt))[..., None]
        m_new = mt[:, -1]
        dec_prev = jnp.exp(b[:, -1] + m - m_new)
        wk = jnp.exp(b[:, -1:] - b + ic - m_new[:, None])
        c = c * dec_prev[..., None, None] + jnp.einsum('bsh,bshd,bshv->bhdv', wk, kc, vc)
        n = n * dec_prev[..., None] + jnp.einsum('bsh,bshd->bhd', wk, kc)
        return (c, n, m_new), h

    xs = (to_chunks(q, cs), to_chunks(k, cs), to_chunks(v, cs), to_chunks(logi, cs), to_chunks(logf, cs))
    (c, n, m), hs = lax.scan(step, (c0, n0, m0), xs)
    return from_chunks(hs), c, n, m


def hgrn_scan(q, k, v, logf, s0):
    cs = math.gcd(q.shape[1], HG_CHUNK)
    mask = jnp.tril(jnp.ones((cs, cs), bool))

    def step(s, inp):
        qc, kc, vc, fc = inp
        b = jnp.cumsum(fc, axis=1)
        seg = b[:, :, None] - b[:, None, :]
        decay = jnp.exp(jnp.where(mask[None, :, :, None, None], seg, -jnp.inf))
        att = jnp.einsum('bthd,bshd,btshd->btsh', qc, kc, decay)
        y = jnp.einsum('btsh,bshv->bthv', att, vc) + jnp.einsum('bthd,bhdv->bthv', qc * jnp.exp(b), s)
        s = s * jnp.exp(b[:, -1])[..., None] + jnp.einsum('bshd,bshv->bhdv', kc * jnp.exp(b[:, -1:] - b), vc)
        return s, y

    xs = (to_chunks(q, cs), to_chunks(k, cs), to_chunks(v, cs), to_chunks(logf, cs))
    s, ys = lax.scan(step, s0, xs)
    return from_chunks(ys), s


def hgrn_lower_bounds(logits):
    p = jax.nn.softmax(logits.astype(jnp.float32), axis=0)
    c = jnp.cumsum(p, axis=0)
    return c - c[0:1]


def token_mixers(u, st, p, lb):
    f32 = jnp.float32
    nb, L = u.shape[0], u.shape[1]
    conv_buf, ssd_h, ml_c, ml_n, ml_m, hg_s = st
    hid = u @ p['w_in']
    (z, xbc, dt_raw, mq, mkey, mval, mo, mi, mf, hq, hf, hi, hgt, gates) = split_cols(hid, IN_SPLITS)

    xbc, conv_new = causal_conv(xbc, conv_buf, p['ssd_conv_w'], p['ssd_conv_b'])
    xbc = jax.nn.silu(xbc)
    xs, bm, cm = split_cols(xbc, (MIX_W, SSD_GROUPS * SSD_STATE, SSD_GROUPS * SSD_STATE))
    hpg = SSD_HEADS // SSD_GROUPS
    xs = xs.astype(f32).reshape(nb, L, SSD_GROUPS, hpg, SSD_HEAD_DIM)
    bm = bm.astype(f32).reshape(nb, L, SSD_GROUPS, SSD_STATE)
    cm = cm.astype(f32).reshape(nb, L, SSD_GROUPS, SSD_STATE)
    dt = jax.nn.softplus((dt_raw + p['ssd_dt_bias']).astype(f32)).reshape(nb, L, SSD_GROUPS, hpg)
    a_neg = -jnp.exp(p['ssd_a_log'].astype(f32)).reshape(SSD_GROUPS, hpg)
    h0 = ssd_h.astype(f32).reshape(nb, SSD_GROUPS, hpg, SSD_HEAD_DIM, SSD_STATE)
    ys, h1 = ssd_scan(xs, dt, dt * a_neg, bm, cm, h0)
    ys = ys + xs * p['ssd_d'].astype(f32).reshape(SSD_GROUPS, hpg, 1)
    ys = ys.reshape(nb, L, MIX_W) * jax.nn.silu(z.astype(f32))
    gw = MIX_W // SSD_GROUPS
    y_ssd = rms_norm(ys.reshape(nb, L, SSD_GROUPS, gw), p['ssd_norm'].reshape(SSD_GROUPS, gw)).reshape(nb, L, MIX_W)

    q = mq.astype(f32).reshape(nb, L, ML_HEADS, ML_DK)
    k = mkey.astype(f32).reshape(nb, L, ML_HEADS, ML_DK) * (ML_DK ** -0.5)
    v = mval.astype(f32).reshape(nb, L, ML_HEADS, ML_DV)
    logi = (mi + p['ml_gate_bias'][0]).astype(f32)
    logf = jax.nn.log_sigmoid((mf + p['ml_gate_bias'][1]).astype(f32))
    hm, c1, n1, m1 = mlstm_scan(q, k, v, logi, logf, ml_c.astype(f32), ml_n.astype(f32), ml_m.astype(f32))
    hm = rms_norm(hm, p['ml_norm'].reshape(ML_HEADS, ML_DV))
    y_ml = (hm * jax.nn.sigmoid(mo.astype(f32)).reshape(nb, L, ML_HEADS, ML_DV)).reshape(nb, L, MIX_W)

    lbh = lb.reshape(HG_HEADS, HG_DK)
    fr = hf.astype(f32).reshape(nb, L, HG_HEADS, HG_DK)
    logf_h = jnp.logaddexp(jnp.log(lbh), jnp.log1p(-lbh) + jax.nn.log_sigmoid(fr))
    k_h = (1.0 - lbh) * jax.nn.sigmoid(-fr)
    q_h = jax.nn.silu(hq.astype(f32)).reshape(nb, L, HG_HEADS, HG_DK)
    v_h = hi.astype(f32).reshape(nb, L, HG_HEADS, HG_DV)
    yh, s1 = hgrn_scan(q_h, k_h, v_h, logf_h, hg_s.astype(f32))
    yh = rms_norm(yh, p['hg_norm'].reshape(HG_HEADS, HG_DV))
    y_hg = (yh * jax.nn.sigmoid(hgt.astype(f32)).reshape(nb, L, HG_HEADS, HG_DV)).reshape(nb, L, MIX_W)

    branches = jnp.stack([y_ssd, y_ml, y_hg], axis=2).astype(u.dtype)
    proj = jnp.einsum('blnc,ncd->blnd', branches, p['w_branch'])
    gate = jax.nn.sigmoid(gates.reshape(nb, L, N_BRANCH, D_MODEL))
    out = jnp.sum(gate * proj, axis=2) @ p['w_mix_out']
    dt_out = u.dtype
    new_st = (conv_new.astype(dt_out),
              h1.reshape(nb, SSD_HEADS, SSD_HEAD_DIM, SSD_STATE).astype(dt_out),
              c1.astype(dt_out), n1.astype(dt_out), m1.astype(dt_out), s1.astype(dt_out))
    return out, new_st


def mem_cross_attention(x, mk, mv, wq, wo):
    nb, L = x.shape[0], x.shape[1]
    q = (x @ wq).reshape(nb, L, X_HEADS, X_HEAD_DIM)
    s = jnp.einsum('blhd,bmhd->bhlm', q, mk.astype(x.dtype)).astype(jnp.float32) * (X_HEAD_DIM ** -0.5)
    pr = jax.nn.softmax(s, axis=-1).astype(x.dtype)
    o = jnp.einsum('bhlm,bmhd->blhd', pr, mv.astype(x.dtype)).reshape(nb, L, D_MODEL)
    return o @ wo


def decoder_layer(x, mk, mv, st, p, lb):
    x = layer_norm(DN_ALPHA * x + 0.5 * swiglu(x, p['ffn_w1'][0], p['ffn_w3'][0], p['ffn_w2'][0]), p['ln_g'][0], p['ln_b'][0])
    mix, new_st = token_mixers(x, st, p, lb)
    x = layer_norm(DN_ALPHA * x + mix, p['ln_g'][1], p['ln_b'][1])
    x = layer_norm(DN_ALPHA * x + mem_cross_attention(x, mk, mv, p['x_wq'], p['x_wo']), p['ln_g'][2], p['ln_b'][2])
    x = layer_norm(DN_ALPHA * x + 0.5 * swiglu(x, p['ffn_w1'][1], p['ffn_w3'][1], p['ffn_w2'][1]), p['ln_g'][3], p['ln_b'][3])
    return x, new_st


def zero_states(nb, dtype):
    return (jnp.zeros((nb, SSD_CONV - 1, SSD_CONV_DIM), dtype),
            jnp.zeros((nb, SSD_HEADS, SSD_HEAD_DIM, SSD_STATE), dtype),
            jnp.zeros((nb, ML_HEADS, ML_DK, ML_DV), dtype),
            jnp.zeros((nb, ML_HEADS, ML_DK), dtype),
            jnp.zeros((nb, ML_HEADS), dtype),
            jnp.zeros((nb, HG_HEADS, HG_DK, HG_DV), dtype))


def setup_inputs(seed: int = 0) -> dict:
    key = jax.random.key(seed)
    keys = iter(jax.random.split(key, 48))

    def nrm(shape, scale):
        return jax.random.normal(next(keys), shape, jnp.float32) * scale

    def unif(shape, lo, hi):
        return jax.random.uniform(next(keys), shape, jnp.float32, lo, hi)

    x_prompt = nrm((BATCH, SEQ, D_MODEL), 1.0)
    x_sample = nrm((DEC_BATCH, DEC_SEQ, D_MODEL), 1.0)
    mem_prompt = nrm((BATCH, MEM_LEN, D_MODEL), 1.0)
    state_ssd_conv = nrm((DEPTH, DEC_BATCH, SSD_CONV - 1, SSD_CONV_DIM), 1.0)
    state_ssd = nrm((DEPTH, DEC_BATCH, SSD_HEADS, SSD_HEAD_DIM, SSD_STATE), 0.5)
    state_mlstm_c = nrm((DEPTH, DEC_BATCH, ML_HEADS, ML_DK, ML_DV), 0.1)
    state_mlstm_n = nrm((DEPTH, DEC_BATCH, ML_HEADS, ML_DK), 0.1)
    state_mlstm_m = unif((DEPTH, DEC_BATCH, ML_HEADS), -1.0, 1.0)
    state_hgrn = nrm((DEPTH, DEC_BATCH, HG_HEADS, HG_DK, HG_DV), 0.3)
    cache_mem_k = nrm((DEPTH, DEC_BATCH, MEM_LEN, X_HEADS, X_HEAD_DIM), 1.0)
    cache_mem_v = nrm((DEPTH, DEC_BATCH, MEM_LEN, X_HEADS, X_HEAD_DIM), DN_BETA)

    ln_g = 1.0 + nrm((DEPTH, 4, D_MODEL), 0.02)
    ln_b = nrm((DEPTH, 4, D_MODEL), 0.02)
    ffn_w1 = nrm((DEPTH, 2, D_MODEL, D_FF), D_MODEL ** -0.5)
    ffn_w3 = nrm((DEPTH, 2, D_MODEL, D_FF), D_MODEL ** -0.5)
    ffn_w2 = nrm((DEPTH, 2, D_FF, D_MODEL), D_FF ** -0.5 * DN_BETA)
    w_in = nrm((DEPTH, D_MODEL, N_IN), D_MODEL ** -0.5)
    ssd_conv_w = nrm((DEPTH, SSD_CONV, SSD_CONV_DIM), SSD_CONV ** -0.5)
    ssd_conv_b = nrm((DEPTH, SSD_CONV_DIM), 0.02)
    dt0 = jnp.exp(unif((DEPTH, SSD_HEADS), math.log(1e-3), math.log(1e-1)))
    ssd_dt_bias = dt0 + jnp.log(-jnp.expm1(-dt0))
    ssd_a_log = jnp.log(unif((DEPTH, SSD_HEADS), 1.0, 16.0))
    ssd_d = 1.0 + nrm((DEPTH, SSD_HEADS), 0.02)
    ssd_norm = 1.0 + nrm((DEPTH, MIX_W), 0.02)
    ml_i_bias = nrm((DEPTH, 1, ML_HEADS), 0.1)
    ml_f_bias = jnp.linspace(3.0, 6.0, ML_HEADS)[None, None, :] + nrm((DEPTH, 1, ML_HEADS), 0.1)
    ml_gate_bias = jnp.concatenate([ml_i_bias, ml_f_bias], axis=1)
    ml_norm = 1.0 + nrm((DEPTH, MIX_W), 0.02)
    hg_lb_logits = nrm((DEPTH, MIX_W), 0.5)
    hg_norm = 1.0 + nrm((DEPTH, MIX_W), 0.02)
    w_branch = nrm((DEPTH, N_BRANCH, MIX_W, D_MODEL), MIX_W ** -0.5)
    w_mix_out = nrm((DEPTH, D_MODEL, D_MODEL), D_MODEL ** -0.5 * DN_BETA)
    x_wq = nrm((DEPTH, D_MODEL, D_MODEL), D_MODEL ** -0.5)
    x_wk = nrm((DEPTH, D_MODEL, D_MODEL), D_MODEL ** -0.5)
    x_wv = nrm((DEPTH, D_MODEL, D_MODEL), D_MODEL ** -0.5 * DN_BETA)
    x_wo = nrm((DEPTH, D_MODEL, D_MODEL), D_MODEL ** -0.5 * DN_BETA)
    return {'x_prompt': x_prompt, 'x_sample': x_sample, 'mem_prompt': mem_prompt,
            'state_ssd_conv': state_ssd_conv, 'state_ssd': state_ssd,
            'state_mlstm_c': state_mlstm_c, 'state_mlstm_n': state_mlstm_n, 'state_mlstm_m': state_mlstm_m,
            'state_hgrn': state_hgrn, 'cache_mem_k': cache_mem_k, 'cache_mem_v': cache_mem_v,
            'ln_g': ln_g, 'ln_b': ln_b, 'ffn_w1': ffn_w1, 'ffn_w3': ffn_w3, 'ffn_w2': ffn_w2,
            'w_in': w_in, 'ssd_conv_w': ssd_conv_w, 'ssd_conv_b': ssd_conv_b, 'ssd_dt_bias': ssd_dt_bias,
            'ssd_a_log': ssd_a_log, 'ssd_d': ssd_d, 'ssd_norm': ssd_norm,
            'ml_gate_bias': ml_gate_bias, 'ml_norm': ml_norm, 'hg_lb_logits': hg_lb_logits, 'hg_norm': hg_norm,
            'w_branch': w_branch, 'w_mix_out': w_mix_out,
            'x_wq': x_wq, 'x_wk': x_wk, 'x_wv': x_wv, 'x_wo': x_wo}


def reference(x_prompt, x_sample, mem_prompt, state_ssd_conv, state_ssd, state_mlstm_c, state_mlstm_n,
              state_mlstm_m, state_hgrn, cache_mem_k, cache_mem_v, ln_g, ln_b, ffn_w1, ffn_w3, ffn_w2,
              w_in, ssd_conv_w, ssd_conv_b, ssd_dt_bias, ssd_a_log, ssd_d, ssd_norm, ml_gate_bias, ml_norm,
              hg_lb_logits, hg_norm, w_branch, w_mix_out, x_wq, x_wk, x_wv, x_wo):
    params = dict(ln_g=ln_g, ln_b=ln_b, ffn_w1=ffn_w1, ffn_w3=ffn_w3, ffn_w2=ffn_w2, w_in=w_in,
                  ssd_conv_w=ssd_conv_w, ssd_conv_b=ssd_conv_b, ssd_dt_bias=ssd_dt_bias, ssd_a_log=ssd_a_log,
                  ssd_d=ssd_d, ssd_norm=ssd_norm, ml_gate_bias=ml_gate_bias, ml_norm=ml_norm, hg_norm=hg_norm,
                  w_branch=w_branch, w_mix_out=w_mix_out, x_wq=x_wq, x_wo=x_wo)
    lbs = hgrn_lower_bounds(hg_lb_logits)

    def run(x, mem_k, mem_v, init):
        new = []
        for l in range(DEPTH):
            p = {name: arr[l] for name, arr in params.items()}
            x, st = decoder_layer(x, mem_k[l], mem_v[l], init(l), p, lbs[l])
            new.append(st)
        stacked = [jnp.stack([s[i] for s in new]) for i in range(6)]
        return x, stacked

    nb = x_prompt.shape[0]
    p_mem_k = jnp.stack([(mem_prompt @ x_wk[l]).reshape(nb, MEM_LEN, X_HEADS, X_HEAD_DIM) for l in range(DEPTH)])
    p_mem_v = jnp.stack([(mem_prompt @ x_wv[l]).reshape(nb, MEM_LEN, X_HEADS, X_HEAD_DIM) for l in range(DEPTH)])
    zeros = zero_states(nb, x_prompt.dtype)
    y_prompt, (p_conv, p_ssd, p_mlstm_c, p_mlstm_n, p_mlstm_m, p_hgrn) = run(
        x_prompt, p_mem_k, p_mem_v, lambda l: zeros)

    y_sample, (s_conv, s_ssd, s_mlstm_c, s_mlstm_n, s_mlstm_m, s_hgrn) = run(
        x_sample, cache_mem_k, cache_mem_v,
        lambda l: (state_ssd_conv[l], state_ssd[l], state_mlstm_c[l], state_mlstm_n[l], state_mlstm_m[l], state_hgrn[l]))

    return (y_prompt, y_sample, p_conv, p_ssd, p_mlstm_c, p_mlstm_n, p_mlstm_m, p_hgrn, p_mem_k, p_mem_v,
            s_conv, s_ssd, s_mlstm_c, s_mlstm_n, s_mlstm_m, s_hgrn)
```

```python
import functools
import math

import jax
import jax.numpy as jnp
from jax import lax
from jax.experimental import pallas as pl
from jax.experimental.pallas import tpu as pltpu

f32 = jnp.float32
bf16 = jnp.bfloat16

D_MODEL = 2048
DEPTH = 2
MIX_W = D_MODEL
SSD_HEAD_DIM = 64
SSD_HEADS = 32
SSD_GROUPS = 4
SSD_STATE = 128
SSD_CONV = 4
SSD_GROUP_W = MIX_W // SSD_GROUPS
SSD_HEADS_PER_GROUP = SSD_HEADS // SSD_GROUPS
ML_HEADS = 4
ML_DV = 512
ML_DK = 256
HG_HEADS = 16
HG_DK = 128
HG_DV = 128
MEM_LEN = 256
X_HEADS = 4
X_HEAD_DIM = 512
D_FF = 5504
D_FF_PAD = 5632
FF_TILE = 512
DN_ALPHA = (2.0 * DEPTH) ** 0.25
LN_EPS = 1e-5
RMS_EPS = 1e-6

LANES = 128
VMEM_LIMIT = 56 * 1024 * 1024

COL_Z = 0
COL_XS = 2048
COL_MV = 4096
COL_MO = 6144
COL_HQ = 8192
COL_HF = 10240
COL_HI = 12288
COL_HG = 14336
COL_GATE = 16384
COL_MQ = 22528
COL_MK = 23552
COL_BM = 24576
COL_CM = 25088
COL_DT = 25600
COL_MG = 26112
N_HID = 26624


def _cp(*sem):
    return pltpu.CompilerParams(dimension_semantics=sem, vmem_limit_bytes=VMEM_LIMIT)


def _dot(a, b):
    return jnp.dot(a, b, preferred_element_type=f32)


def _dot_nt(a, b):
    return lax.dot_general(a, b, (((1,), (1,)), ((), ())), preferred_element_type=f32)


def _dot_tn(a, b):
    return lax.dot_general(a, b, (((0,), (0,)), ((), ())), preferred_element_type=f32)


def _dot_exact(a, b):
    return jnp.dot(a, b, precision=lax.Precision.HIGHEST, preferred_element_type=f32)


def _sigmoid(x):
    return 1.0 / (1.0 + jnp.exp(-x))


def _silu(x):
    return x * _sigmoid(x)


def _softplus(x):
    return jnp.maximum(x, 0.0) + jnp.log1p(jnp.exp(-jnp.abs(x)))


def _log_sigmoid(x):
    return jnp.minimum(x, 0.0) - jnp.log1p(jnp.exp(-jnp.abs(x)))


def _layer_norm(y, g, b):
    mu = jnp.mean(y, axis=-1, keepdims=True)
    yc = y - mu
    var = jnp.mean(yc * yc, axis=-1, keepdims=True)
    return yc * lax.rsqrt(var + LN_EPS) * g + b


def _rms_norm(y, g):
    return y * lax.rsqrt(jnp.mean(y * y, axis=-1, keepdims=True) + RMS_EPS) * g


def _row_tile(m, preferred):
    t = preferred
    while m % t:
        t //= 2
    return t


def _tril(n):
    r = lax.broadcasted_iota(jnp.int32, (n, n), 0)
    c = lax.broadcasted_iota(jnp.int32, (n, n), 1)
    return r >= c


def _ffn_kernel(x_ref, w1_ref, w3_ref, w2_ref, g_ref, b_ref, o_ref, xb_ref, acc_ref):
    j = pl.program_id(1)

    @pl.when(j == 0)
    def _():
        xb_ref[...] = x_ref[...].astype(bf16)
        acc_ref[...] = jnp.zeros_like(acc_ref)

    xb = xb_ref[...]
    a = _dot(xb, w1_ref[...])
    b = _dot(xb, w3_ref[...])
    h = (_silu(a) * b).astype(bf16)
    acc_ref[...] += _dot(h, w2_ref[...])

    @pl.when(j == pl.num_programs(1) - 1)
    def _():
        y = DN_ALPHA * x_ref[...] + 0.5 * acc_ref[...]
        o_ref[...] = _layer_norm(y, g_ref[...], b_ref[...])


def _ffn_ln(x, w1p, w3p, w2p, ln_g, ln_b, l, k, ln_i, tm=512):
    m = x.shape[0]
    tm = _row_tile(m, tm)
    tf = FF_TILE
    return pl.pallas_call(
        _ffn_kernel,
        out_shape=jax.ShapeDtypeStruct((m, D_MODEL), f32),
        grid=(m // tm, D_FF_PAD // tf),
        in_specs=[
            pl.BlockSpec((tm, D_MODEL), lambda i, j: (i, 0)),
            pl.BlockSpec((None, None, D_MODEL, tf), lambda i, j: (l, k, 0, j)),
            pl.BlockSpec((None, None, D_MODEL, tf), lambda i, j: (l, k, 0, j)),
            pl.BlockSpec((None, None, tf, D_MODEL), lambda i, j: (l, k, j, 0)),
            pl.BlockSpec((None, None, 1, D_MODEL), lambda i, j: (l, ln_i, 0, 0)),
            pl.BlockSpec((None, None, 1, D_MODEL), lambda i, j: (l, ln_i, 0, 0)),
        ],
        out_specs=pl.BlockSpec((tm, D_MODEL), lambda i, j: (i, 0)),
        scratch_shapes=[pltpu.VMEM((tm, D_MODEL), bf16), pltpu.VMEM((tm, D_MODEL), f32)],
        compiler_params=_cp("parallel", "arbitrary"),
        name="ffn_ln",
    )(x, w1p, w3p, w2p, ln_g, ln_b)


def _mm_kernel(x_ref, w_ref, o_ref, xb_ref):
    @pl.when(pl.program_id(1) == 0)
    def _():
        xb_ref[...] = x_ref[...].astype(bf16)

    o_ref[...] = _dot(xb_ref[...], w_ref[...]).astype(o_ref.dtype)


def _matmul(x, w, l, out_dtype, tm, tn=512, name="matmul"):
    m, kdim = x.shape
    n = w.shape[-1]
    tm = _row_tile(m, tm)
    return pl.pallas_call(
        _mm_kernel,
        out_shape=jax.ShapeDtypeStruct((m, n), out_dtype),
        grid=(m // tm, n // tn),
        in_specs=[
            pl.BlockSpec((tm, kdim), lambda i, j: (i, 0)),
            pl.BlockSpec((None, kdim, tn), lambda i, j: (l, 0, j)),
        ],
        out_specs=pl.BlockSpec((tm, tn), lambda i, j: (i, j)),
        scratch_shapes=[pltpu.VMEM((tm, kdim), bf16)],
        compiler_params=_cp("parallel", "arbitrary"),
        name=name,
    )(x, w)


def _res_ln_kernel(x_ref, y_ref, w_ref, g_ref, b_ref, o_ref):
    r = _dot(y_ref[...].astype(bf16), w_ref[...])
    o_ref[...] = _layer_norm(DN_ALPHA * x_ref[...] + r, g_ref[...], b_ref[...])


def _res_ln(x, y, w, ln_g, ln_b, l, ln_i, tm=512):
    m = x.shape[0]
    tm = _row_tile(m, tm)
    return pl.pallas_call(
        _res_ln_kernel,
        out_shape=jax.ShapeDtypeStruct((m, D_MODEL), f32),
        grid=(m // tm,),
        in_specs=[
            pl.BlockSpec((tm, D_MODEL), lambda i: (i, 0)),
            pl.BlockSpec((tm, D_MODEL), lambda i: (i, 0)),
            pl.BlockSpec((None, D_MODEL, D_MODEL), lambda i: (l, 0, 0)),
            pl.BlockSpec((None, None, 1, D_MODEL), lambda i: (l, ln_i, 0, 0)),
            pl.BlockSpec((None, None, 1, D_MODEL), lambda i: (l, ln_i, 0, 0)),
        ],
        out_specs=pl.BlockSpec((tm, D_MODEL), lambda i: (i, 0)),
        compiler_params=_cp("parallel"),
        name="res_ln",
    )(x, y, w, ln_g, ln_b)


def _merge_kernel(y0_ref, y1_ref, y2_ref, g0_ref, g1_ref, g2_ref, w_ref, o_ref):
    acc = _sigmoid(g0_ref[...]) * _dot(y0_ref[...].astype(bf16), w_ref[0])
    acc += _sigmoid(g1_ref[...]) * _dot(y1_ref[...].astype(bf16), w_ref[1])
    acc += _sigmoid(g2_ref[...]) * _dot(y2_ref[...].astype(bf16), w_ref[2])
    o_ref[...] = acc.astype(o_ref.dtype)


def _merge(ys, hid, wb, l, tm=512, tn=512):
    m = hid.shape[0]
    tm = _row_tile(m, tm)
    gb = COL_GATE // tn
    nb = D_MODEL // tn
    y_spec = pl.BlockSpec((tm, MIX_W), lambda i, j: (i, 0))
    return pl.pallas_call(
        _merge_kernel,
        out_shape=jax.ShapeDtypeStruct((m, D_MODEL), bf16),
        grid=(m // tm, D_MODEL // tn),
        in_specs=[
            y_spec, y_spec, y_spec,
            pl.BlockSpec((tm, tn), lambda i, j: (i, gb + j)),
            pl.BlockSpec((tm, tn), lambda i, j: (i, gb + nb + j)),
            pl.BlockSpec((tm, tn), lambda i, j: (i, gb + 2 * nb + j)),
            pl.BlockSpec((None, 3, MIX_W, tn), lambda i, j: (l, 0, 0, j)),
        ],
        out_specs=pl.BlockSpec((tm, tn), lambda i, j: (i, j)),
        compiler_params=_cp("parallel", "arbitrary"),
        name="merge",
    )(ys[0], ys[1], ys[2], hid, hid, hid, wb)


def _attn_kernel(q_ref, k_ref, v_ref, o_ref, *, nseq, rows):
    scale = X_HEAD_DIM ** -0.5

    def seq_body(g, carry):
        rs = pl.ds(pl.multiple_of(g * rows, 8), rows)
        for h in range(X_HEADS):
            cs = slice(h * X_HEAD_DIM, (h + 1) * X_HEAD_DIM)
            q = q_ref[rs, cs].astype(bf16)
            kh = k_ref[g, :, cs].astype(bf16)
            vh = v_ref[g, :, cs].astype(bf16)
            s = _dot_nt(q, kh) * scale
            s = s - jnp.max(s, axis=-1, keepdims=True)
            e = jnp.exp(s)
            p = e / jnp.sum(e, axis=-1, keepdims=True)
            o_ref[rs, cs] = _dot(p.astype(bf16), vh).astype(o_ref.dtype)
        return carry

    lax.fori_loop(0, nseq, seq_body, 0)


def _attention(q, mk, mv, l, row0, nb, seq_len, nseq, rows):
    r = nseq * rows
    steps_per_seq = seq_len // rows if nseq == 1 else 1
    rb0 = row0 // r
    if mk.ndim == 4:
        kv_spec = pl.BlockSpec((None, nseq, MEM_LEN, D_MODEL), lambda b, t: (l, b, 0, 0))
    else:
        kv_spec = pl.BlockSpec((nseq, MEM_LEN, D_MODEL), lambda b, t: (b, 0, 0))
    return pl.pallas_call(
        functools.partial(_attn_kernel, nseq=nseq, rows=rows),
        out_shape=jax.ShapeDtypeStruct((nb * seq_len, D_MODEL), f32),
        grid=(nb // nseq, steps_per_seq),
        in_specs=[
            pl.BlockSpec((r, D_MODEL), lambda b, t: (rb0 + b * steps_per_seq + t, 0)),
            kv_spec, kv_spec,
        ],
        out_specs=pl.BlockSpec((r, D_MODEL), lambda b, t: (b * steps_per_seq + t, 0)),
        compiler_params=_cp("parallel", "arbitrary"),
        name="mem_attn",
    )(q, mk, mv)


def _ssd_kernel(z_ref, xs_ref, bm_ref, cm_ref, dt_ref, cx0_ref, cb0_ref, cc0_ref, h0_ref,
                wx_ref, wb_ref, wc_ref, bx_ref, bb_ref, bc_ref, hp_ref, nrm_ref,
                y_ref, cxo_ref, cbo_ref, cco_ref, ho_ref,
                upx_ref, upb_ref, upc_ref, yg_ref, *, cs, nseq, per_seq_state):
    t_idx = pl.program_id(2)

    @pl.when(t_idx == 0)
    def _():
        ho_ref[...] = h0_ref[...]
        cxo_ref[...] = cx0_ref[...]
        cbo_ref[...] = cb0_ref[...]
        cco_ref[...] = cc0_ref[...]

    tri = _tril(cs)
    tril_f = tri.astype(f32)
    lane = lax.broadcasted_iota(jnp.int32, (cs, LANES), 1)
    left = lane < SSD_HEAD_DIM
    srow = lax.broadcasted_iota(jnp.int32, (LANES, LANES), 0)
    top = srow < SSD_HEAD_DIM
    hp = hp_ref[...]
    a_neg = -jnp.exp(hp[1:2, :])
    gw = SSD_GROUP_W

    def conv(up_ref, st_ref, u_ref, w_ref, b_ref, slot, rs):
        up_ref[5:8, :] = st_ref[slot]
        up_ref[8:8 + cs, :] = u_ref[rs, :]
        out = b_ref[...] + up_ref[5:5 + cs, :] * w_ref[0:1, :]
        for j in range(1, SSD_CONV):
            out = out + up_ref[5 + j:5 + j + cs, :] * w_ref[j:j + 1, :]
        st_ref[slot] = up_ref[5 + cs:8 + cs, :]
        return _silu(out)

    def seq_body(i, carry):
        slot = i if per_seq_state else 0
        rs = pl.ds(pl.multiple_of(i * cs, 8), cs)
        xc = conv(upx_ref, cxo_ref, xs_ref, wx_ref, bx_ref, slot, rs)
        bmat = conv(upb_ref, cbo_ref, bm_ref, wb_ref, bb_ref, slot, rs)
        cmat = conv(upc_ref, cco_ref, cm_ref, wc_ref, bc_ref, slot, rs)
        bmb = bmat.astype(bf16)
        cmb = cmat.astype(bf16)

        dt = _softplus(dt_ref[rs, :] + hp[0:1, :])
        a = dt * a_neg
        acum = _dot_exact(tril_f, a)
        acum_t = acum.T
        dt_t = dt.T
        a_last = acum[cs - 1:cs, :]
        ea = jnp.exp(acum)
        tail = jnp.exp(a_last - acum) * dt
        e_last = jnp.exp(a_last)
        cb = _dot_nt(cmb, bmb)

        for pp in range(SSD_HEADS_PER_GROUP // 2):
            j0, j1 = 2 * pp, 2 * pp + 1
            ls = slice(pp * LANES, (pp + 1) * LANES)
            xp = xc[:, ls]
            xpb = xp.astype(bf16)

            def head_w(j):
                seg = acum[:, j:j + 1] - acum_t[j:j + 1, :]
                dec = jnp.exp(jnp.where(tri, seg, -jnp.inf))
                return (cb * dec * dt_t[j:j + 1, :]).astype(bf16)

            zero = jnp.zeros_like(xpb)
            y = _dot(head_w(j0), jnp.where(left, xpb, zero)) + _dot(head_w(j1), jnp.where(left, zero, xpb))
            hstate = ho_ref[slot, pp * LANES:(pp + 1) * LANES, :]
            y = y + _dot_nt(cmb, hstate.astype(bf16)) * jnp.where(left, ea[:, j0:j0 + 1], ea[:, j1:j1 + 1])
            xw = (xp * jnp.where(left, tail[:, j0:j0 + 1], tail[:, j1:j1 + 1])).astype(bf16)
            dec_rows = jnp.where(top, e_last[:, j0:j0 + 1], e_last[:, j1:j1 + 1])
            ho_ref[slot, pp * LANES:(pp + 1) * LANES, :] = hstate * dec_rows + _dot_tn(xw, bmb)
            d_skip = jnp.where(left, hp[2:3, j0:j0 + 1], hp[2:3, j1:j1 + 1])
            y = (y + xp * d_skip) * _silu(z_ref[rs, ls])
            yg_ref[:, ls] = y

        yg = yg_ref[...]
        y_ref[rs, :] = _rms_norm(yg, nrm_ref[...]).astype(y_ref.dtype)
        return carry

    lax.fori_loop(0, nseq, seq_body, 0)


def _ssd(hid, conv_st, h_st, l, st_l, conv_w, conv_b, hp, nrm, *, row0, nb, seq_len, cs, nseq, rows):
    per_seq_state = nb > 1 and seq_len == cs
    nt = seq_len // rows if not per_seq_state else 1
    g = nseq if per_seq_state else 1
    rb0 = row0 // rows
    gw = SSD_GROUP_W

    def rowblk(b, t):
        return rb0 + b * nt + t

    def hid_spec(col0, width):
        return pl.BlockSpec((rows, width), lambda b, gi, t: (rowblk(b, t), col0 // width + gi))

    def st_spec(col0, width):
        return pl.BlockSpec((None, g, 3, width), lambda b, gi, t: (st_l, b, 0, col0 // width + gi))

    def w_spec(nrow, col0, width):
        return pl.BlockSpec((None, nrow, width), lambda b, gi, t: (l, 0, col0 // width + gi))

    def sto_spec(width):
        return pl.BlockSpec((g, 3, width), lambda b, gi, t: (b, 0, gi))

    kern = functools.partial(_ssd_kernel, cs=cs, nseq=nseq, per_seq_state=per_seq_state)
    return pl.pallas_call(
        kern,
        out_shape=(jax.ShapeDtypeStruct((nb * seq_len, MIX_W), f32),
                   jax.ShapeDtypeStruct((nb, 3, MIX_W), f32),
                   jax.ShapeDtypeStruct((nb, 3, SSD_GROUPS * SSD_STATE), f32),
                   jax.ShapeDtypeStruct((nb, 3, SSD_GROUPS * SSD_STATE), f32),
                   jax.ShapeDtypeStruct((nb, SSD_GROUPS, gw, SSD_STATE), f32)),
        grid=(nb // g, SSD_GROUPS, nt),
        in_specs=[
            hid_spec(COL_Z, gw), hid_spec(COL_XS, gw), hid_spec(COL_BM, LANES), hid_spec(COL_CM, LANES),
            hid_spec(COL_DT, LANES),
            st_spec(0, gw), st_spec(MIX_W, LANES), st_spec(MIX_W + SSD_GROUPS * SSD_STATE, LANES),
            pl.BlockSpec((None, g, None, gw, SSD_STATE), lambda b, gi, t: (st_l, b, gi, 0, 0)),
            w_spec(SSD_CONV, 0, gw), w_spec(SSD_CONV, MIX_W, LANES),
            w_spec(SSD_CONV, MIX_W + SSD_GROUPS * SSD_STATE, LANES),
            w_spec(1, 0, gw), w_spec(1, MIX_W, LANES), w_spec(1, MIX_W + SSD_GROUPS * SSD_STATE, LANES),
            pl.BlockSpec((None, None, 8, LANES), lambda b, gi, t: (l, gi, 0, 0)),
            pl.BlockSpec((None, 1, gw), lambda b, gi, t: (l, 0, gi)),
        ],
        out_specs=(
            pl.BlockSpec((rows, gw), lambda b, gi, t: (b * nt + t, gi)),
            sto_spec(gw), sto_spec(LANES), sto_spec(LANES),
            pl.BlockSpec((g, None, gw, SSD_STATE), lambda b, gi, t: (b, gi, 0, 0)),
        ),
        scratch_shapes=[pltpu.VMEM((cs + 8, gw), f32), pltpu.VMEM((cs + 8, LANES), f32),
                        pltpu.VMEM((cs + 8, LANES), f32), pltpu.VMEM((cs, gw), f32)],
        compiler_params=_cp("parallel", "parallel", "arbitrary"),
        name="ssd_scan",
    )(hid, hid, hid, hid, hid, conv_st, conv_st, conv_st, h_st,
      conv_w, conv_w, conv_w, conv_b, conv_b, conv_b, hp, nrm)


def _mlstm_kernel(q_ref, k_ref, v_ref, o_ref, gt_ref, c0_ref, n0_ref, m0_ref, gb_ref, nrm_ref,
                  y_ref, co_ref, no_ref, mo_ref, *, cs, nseq, per_seq_state):
    t_idx = pl.program_id(2)

    @pl.when(t_idx == 0)
    def _():
        co_ref[...] = c0_ref[...]
        no_ref[...] = n0_ref[...]
        mo_ref[...] = m0_ref[...]

    tri = _tril(cs)
    tril_f = tri.astype(f32)
    kscale = ML_DK ** -0.5

    def seq_body(i, carry):
        slot = i if per_seq_state else 0
        rs = pl.ds(pl.multiple_of(i * cs, 8), cs)
        gates = gt_ref[rs, :] + gb_ref[...]
        bfull = _dot_exact(tril_f, _log_sigmoid(gates))
        gates_t = gates.T
        bfull_t = bfull.T
        logi_c = gates[:, 0:1]
        b_c = bfull[:, 1:2]
        logi_r = gates_t[0:1, :]
        b_r = bfull_t[1:2, :]
        m_prev = mo_ref[slot]

        dmat = jnp.where(tri, b_c - b_r + logi_r, -jnp.inf)
        prev = b_c + m_prev
        mt = jnp.maximum(prev, jnp.max(dmat, axis=-1, keepdims=True))
        wts = jnp.exp(dmat - mt)
        sprev = jnp.exp(prev - mt)

        qh = q_ref[rs, :]
        kh = k_ref[rs, :] * kscale
        qb = qh.astype(bf16)
        kb = kh.astype(bf16)
        vb = v_ref[rs, :].astype(bf16)
        cst = co_ref[slot]
        nrow = no_ref[slot]
        qk = _dot_nt(qb, kb) * wts
        num = _dot(qk.astype(bf16), vb) + _dot(qb, cst.astype(bf16)) * sprev
        den = jnp.sum(qk, axis=-1, keepdims=True) + jnp.sum(qh * nrow, axis=-1, keepdims=True) * sprev
        hh = num / jnp.maximum(jnp.abs(den), jnp.exp(-mt))

        m_new = mt[cs - 1:cs, :]
        b_last = b_c[cs - 1:cs, :]
        dec_prev = jnp.exp(b_last + m_prev - m_new)
        wk = jnp.exp(b_last - b_c + logi_c - m_new)
        kw = kh * wk
        co_ref[slot] = cst * dec_prev + _dot_tn(kw.astype(bf16), vb)
        no_ref[slot] = nrow * dec_prev + jnp.sum(kw, axis=0, keepdims=True)
        mo_ref[slot] = m_new

        y = _rms_norm(hh, nrm_ref[...]) * _sigmoid(o_ref[rs, :])
        y_ref[rs, :] = y.astype(y_ref.dtype)
        return carry

    lax.fori_loop(0, nseq, seq_body, 0)


def _mlstm(hid, c_st, n_st, m_st, l, st_l, gbias, nrm, *, row0, nb, seq_len, cs, nseq, rows):
    per_seq_state = nb > 1 and seq_len == cs
    nt = seq_len // rows if not per_seq_state else 1
    g = nseq if per_seq_state else 1
    rb0 = row0 // rows

    def hid_spec(col0, width):
        return pl.BlockSpec((rows, width), lambda b, h, t: (rb0 + b * nt + t, col0 // width + h))

    kern = functools.partial(_mlstm_kernel, cs=cs, nseq=nseq, per_seq_state=per_seq_state)
    return pl.pallas_call(
        kern,
        out_shape=(jax.ShapeDtypeStruct((nb * seq_len, MIX_W), f32),
                   jax.ShapeDtypeStruct((nb, ML_HEADS, ML_DK, ML_DV), f32),
                   jax.ShapeDtypeStruct((nb, ML_HEADS, 1, ML_DK), f32),
                   jax.ShapeDtypeStruct((nb, ML_HEADS, 1, 1), f32)),
        grid=(nb // g, ML_HEADS, nt),
        in_specs=[
            hid_spec(COL_MQ, ML_DK), hid_spec(COL_MK, ML_DK), hid_spec(COL_MV, ML_DV), hid_spec(COL_MO, ML_DV),
            hid_spec(COL_MG, LANES),
            pl.BlockSpec((None, g, None, ML_DK, ML_DV), lambda b, h, t: (st_l, b, h, 0, 0)),
            pl.BlockSpec((None, g, None, 1, ML_DK), lambda b, h, t: (st_l, b, h, 0, 0)),
            pl.BlockSpec((None, g, None, 1, 1), lambda b, h, t: (st_l, b, h, 0, 0)),
            pl.BlockSpec((None, None, 1, LANES), lambda b, h, t: (l, h, 0, 0)),
            pl.BlockSpec((None, 1, ML_DV), lambda b, h, t: (l, 0, h)),
        ],
        out_specs=(
            pl.BlockSpec((rows, ML_DV), lambda b, h, t: (b * nt + t, h)),
            pl.BlockSpec((g, None, ML_DK, ML_DV), lambda b, h, t: (b, h, 0, 0)),
            pl.BlockSpec((g, None, 1, ML_DK), lambda b, h, t: (b, h, 0, 0)),
            pl.BlockSpec((g, None, 1, 1), lambda b, h, t: (b, h, 0, 0)),
        ),
        compiler_params=_cp("parallel", "parallel", "arbitrary"),
        name="mlstm_scan",
    )(hid, hid, hid, hid, hid, c_st, n_st, m_st, gbias, nrm)


def _hgrn_kernel(q_ref, f_ref, i_ref, g_ref, s0_ref, lg_ref, nrm_ref, y_ref, so_ref,
                 b_ref, k_ref, *, cs, sub, nseq, per_seq_state, layer):
    t_idx = pl.program_id(2)

    @pl.when(t_idx == 0)
    def _():
        so_ref[...] = s0_ref[...]

    lg = lg_ref[...]
    pe = jnp.exp(lg - jnp.max(lg, axis=0, keepdims=True))
    pw = pe / jnp.sum(pe, axis=0, keepdims=True)
    lb = jnp.zeros((1, HG_DK), f32)
    for j in range(1, layer + 1):
        lb = lb + pw[j:j + 1, :]
    log_lb = jnp.log(lb)
    log_1m = jnp.log1p(-lb)

    tri_f = _tril(cs).astype(f32)
    nsub = cs // sub
    row_c = lax.broadcasted_iota(jnp.int32, (cs, HG_DK), 0)
    row_s = lax.broadcasted_iota(jnp.int32, (sub, HG_DK), 0)
    lane_s = lax.broadcasted_iota(jnp.int32, (sub, cs), 1)

    def seq_body(i, carry):
        slot = i if per_seq_state else 0
        r0 = pl.multiple_of(i * cs, 8)
        rs = pl.ds(r0, cs)
        fr = f_ref[rs, :]
        u = log_1m + _log_sigmoid(fr)
        mx = jnp.maximum(log_lb, u)
        mn = jnp.minimum(log_lb, u)
        logf = mx + jnp.log1p(jnp.exp(mn - mx))
        kk = (1.0 - lb) * _sigmoid(-fr)
        qq = _silu(q_ref[rs, :])
        vb = i_ref[rs, :].astype(bf16)
        b = _dot_exact(tri_f, logf)
        b_ref[...] = b
        k_ref[...] = kk
        st = so_ref[slot]

        y_inter = _dot((qq * jnp.exp(b)).astype(bf16), st.astype(bf16))

        y_parts = []
        for si in range(nsub):
            s0 = si * sub
            q_s = qq[s0:s0 + sub, :]
            b_s = b[s0:s0 + sub, :]
            if si > 0:
                ref_row = b[s0 - 1:s0, :]
                q_t = (q_s * jnp.exp(b_s - ref_row)).astype(bf16)
                k_t = (kk * jnp.exp(jnp.where(row_c < s0, ref_row - b, -jnp.inf))).astype(bf16)
                att = _dot_nt(q_t, k_t)
            else:
                att = jnp.zeros((sub, cs), f32)

            def col_body(s, att):
                r = s0 + s
                bs = b_ref[pl.ds(r, 1), :]
                ks = k_ref[pl.ds(r, 1), :]
                e = jnp.exp(jnp.where(row_s >= s, b_s - bs, -jnp.inf))
                col = jnp.sum(q_s * e * ks, axis=-1, keepdims=True)
                return jnp.where(lane_s == r, col, att)

            att = lax.fori_loop(0, sub, col_body, att, unroll=8)
            y_parts.append(_dot(att.astype(bf16), vb))
        y_intra = y_parts[0] if nsub == 1 else jnp.concatenate(y_parts, axis=0)
        yh = y_inter + y_intra

        b_last = b[cs - 1:cs, :]
        kdec = (kk * jnp.exp(b_last - b)).astype(bf16)
        dcol = jnp.exp(jnp.broadcast_to(b_last, (8, HG_DK))).T[:, 0:1]
        so_ref[slot] = st * dcol + _dot_tn(kdec, vb)

        y = _rms_norm(yh, nrm_ref[...]) * _sigmoid(g_ref[rs, :])
        y_ref[rs, :] = y.astype(y_ref.dtype)
        return carry

    lax.fori_loop(0, nseq, seq_body, 0)


def _hgrn(hid, s_st, l, st_l, logits, nrm, *, row0, nb, seq_len, cs, sub, nseq, rows):
    per_seq_state = nb > 1 and seq_len == cs
    nt = seq_len // rows if not per_seq_state else 1
    g = nseq if per_seq_state else 1
    rb0 = row0 // rows

    def hid_spec(col0):
        return pl.BlockSpec((rows, LANES), lambda b, h, t: (rb0 + b * nt + t, col0 // LANES + h))

    kern = functools.partial(_hgrn_kernel, cs=cs, sub=sub, nseq=nseq, per_seq_state=per_seq_state, layer=l)
    return pl.pallas_call(
        kern,
        out_shape=(jax.ShapeDtypeStruct((nb * seq_len, MIX_W), f32),
                   jax.ShapeDtypeStruct((nb, HG_HEADS, HG_DK, HG_DV), f32)),
        grid=(nb // g, HG_HEADS, nt),
        in_specs=[
            hid_spec(COL_HQ), hid_spec(COL_HF), hid_spec(COL_HI), hid_spec(COL_HG),
            pl.BlockSpec((None, g, None, HG_DK, HG_DV), lambda b, h, t: (st_l, b, h, 0, 0)),
            pl.BlockSpec((DEPTH, LANES), lambda b, h, t: (0, h)),
            pl.BlockSpec((None, 1, LANES), lambda b, h, t: (l, 0, h)),
        ],
        out_specs=(
            pl.BlockSpec((rows, LANES), lambda b, h, t: (b * nt + t, h)),
            pl.BlockSpec((g, None, HG_DK, HG_DV), lambda b, h, t: (b, h, 0, 0)),
        ),
        scratch_shapes=[pltpu.VMEM((cs, HG_DK), f32), pltpu.VMEM((cs, HG_DK), f32)],
        compiler_params=_cp("parallel", "parallel", "arbitrary"),
        name="hgrn_scan",
    )(hid, hid, hid, hid, s_st, logits, nrm)


def _prep_w_in(w_in):
    off = {}
    start = 0
    names = ("z", "xbc", "dt", "mq", "mk", "mv", "mo", "mi", "mf", "hq", "hf", "hi", "hg", "gate")
    sizes = (MIX_W, MIX_W + 2 * SSD_GROUPS * SSD_STATE, SSD_HEADS, ML_HEADS * ML_DK, ML_HEADS * ML_DK, MIX_W, MIX_W,
             ML_HEADS, ML_HEADS, MIX_W, MIX_W, MIX_W, MIX_W, 3 * D_MODEL)
    for nme, sz in zip(names, sizes):
        off[nme] = (start, sz)
        start += sz

    def seg(nme, lo=0, hi=None):
        s, sz = off[nme]
        hi = sz if hi is None else hi
        return w_in[:, :, s + lo:s + hi]

    nbc = SSD_GROUPS * SSD_STATE
    dpth, kdim = w_in.shape[0], w_in.shape[1]
    dt = seg("dt").reshape(dpth, kdim, SSD_GROUPS, SSD_HEADS_PER_GROUP)
    dt = jnp.pad(dt, ((0, 0), (0, 0), (0, 0), (0, LANES - SSD_HEADS_PER_GROUP))).reshape(dpth, kdim, SSD_GROUPS * LANES)
    mg = jnp.stack([seg("mi"), seg("mf")], axis=-1)
    mg = jnp.pad(mg, ((0, 0), (0, 0), (0, 0), (0, LANES - 2))).reshape(dpth, kdim, ML_HEADS * LANES)
    cols = [seg("z"), seg("xbc", 0, MIX_W), seg("mv"), seg("mo"), seg("hq"), seg("hf"), seg("hi"), seg("hg"),
            seg("gate"), seg("mq"), seg("mk"), seg("xbc", MIX_W, MIX_W + nbc), seg("xbc", MIX_W + nbc, MIX_W + 2 * nbc),
            dt, mg]
    return jnp.concatenate(cols, axis=-1).astype(bf16)


def _row_major_groups(x_prompt, x_sample):
    return jnp.concatenate([x_prompt.reshape(-1, D_MODEL), x_sample.reshape(-1, D_MODEL)], axis=0)


def kernel(x_prompt, x_sample, mem_prompt, state_ssd_conv, state_ssd, state_mlstm_c, state_mlstm_n, state_mlstm_m, state_hgrn, cache_mem_k, cache_mem_v, ln_g, ln_b, ffn_w1, ffn_w3, ffn_w2, w_in, ssd_conv_w, ssd_conv_b, ssd_dt_bias, ssd_a_log, ssd_d, ssd_norm, ml_gate_bias, ml_norm, hg_lb_logits, hg_norm, w_branch, w_mix_out, x_wq, x_wk, x_wv, x_wo):
    nbp, lp = x_prompt.shape[0], x_prompt.shape[1]
    nbs, ls = x_sample.shape[0], x_sample.shape[1]
    mp = nbp * lp
    depth = w_in.shape[0]

    ffpad = D_FF_PAD - D_FF
    w1p = jnp.pad(ffn_w1.astype(bf16), ((0, 0), (0, 0), (0, 0), (0, ffpad)))
    w3p = jnp.pad(ffn_w3.astype(bf16), ((0, 0), (0, 0), (0, 0), (0, ffpad)))
    w2p = jnp.pad(ffn_w2.astype(bf16), ((0, 0), (0, 0), (0, ffpad), (0, 0)))
    w_hid = _prep_w_in(w_in)
    wb = w_branch.astype(bf16)
    w_mo = w_mix_out.astype(bf16)
    w_q = x_wq.astype(bf16)
    w_o = x_wo.astype(bf16)
    w_kv = jnp.concatenate([x_wk, x_wv], axis=-1).astype(bf16)
    lng = ln_g.reshape(depth, 4, 1, D_MODEL)
    lnb = ln_b.reshape(depth, 4, 1, D_MODEL)

    def per_head_rows(*rows):
        r = jnp.stack([a.reshape(depth, SSD_GROUPS, SSD_HEADS_PER_GROUP) for a in rows], axis=2)
        return jnp.pad(r, ((0, 0), (0, 0), (0, 8 - len(rows)), (0, LANES - SSD_HEADS_PER_GROUP)))

    ssd_hp = per_head_rows(ssd_dt_bias, ssd_a_log, ssd_d)
    ssd_nrm = ssd_norm.reshape(depth, 1, MIX_W)
    conv_b = ssd_conv_b.reshape(depth, 1, -1)
    ml_gb = jnp.pad(jnp.swapaxes(ml_gate_bias, 1, 2), ((0, 0), (0, 0), (0, LANES - 2))).reshape(depth, ML_HEADS, 1, LANES)
    ml_nrm = ml_norm.reshape(depth, 1, MIX_W)
    hg_nrm = hg_norm.reshape(depth, 1, MIX_W)

    s_ssd = state_ssd.reshape(depth, nbs, SSD_GROUPS, SSD_GROUP_W, SSD_STATE)
    s_n = state_mlstm_n.reshape(depth, nbs, ML_HEADS, 1, ML_DK)
    s_m = state_mlstm_m.reshape(depth, nbs, ML_HEADS, 1, 1)
    cache_k = cache_mem_k.reshape(depth, nbs, MEM_LEN, D_MODEL)
    cache_v = cache_mem_v.reshape(depth, nbs, MEM_LEN, D_MODEL)
    z_conv = jnp.zeros((1, nbp, SSD_CONV - 1, state_ssd_conv.shape[-1]), f32)
    z_ssd = jnp.zeros((1, nbp, SSD_GROUPS, SSD_GROUP_W, SSD_STATE), f32)
    z_c = jnp.zeros((1, nbp, ML_HEADS, ML_DK, ML_DV), f32)
    z_n = jnp.zeros((1, nbp, ML_HEADS, 1, ML_DK), f32)
    z_m = jnp.zeros((1, nbp, ML_HEADS, 1, 1), f32)
    z_hg = jnp.zeros((1, nbp, HG_HEADS, HG_DK, HG_DV), f32)

    mem_rows = mem_prompt.reshape(nbp * MEM_LEN, D_MODEL)
    x = _row_major_groups(x_prompt, x_sample)

    p_states = [[] for _ in range(8)]
    s_states = [[] for _ in range(6)]
    pc, sc = 128, ls

    for l in range(depth):
        x = _ffn_ln(x, w1p, w3p, w2p, lng, lnb, l, 0, 0)
        hid = _matmul(x, w_hid, l, f32, tm=1024, name="in_proj")

        yp, cx, cb_, cc, hs = _ssd(hid, z_conv, z_ssd, l, 0, ssd_conv_w, conv_b, ssd_hp, ssd_nrm,
                                   row0=0, nb=nbp, seq_len=lp, cs=pc, nseq=2, rows=2 * pc)
        ysm, sx, sb, scc, shs = _ssd(hid, state_ssd_conv, s_ssd, l, l, ssd_conv_w, conv_b, ssd_hp, ssd_nrm,
                                     row0=mp, nb=nbs, seq_len=ls, cs=sc, nseq=8, rows=8 * sc)
        y_ssd = jnp.concatenate([yp, ysm], axis=0)
        p_states[0].append(jnp.concatenate([cx, cb_, cc], axis=-1))
        p_states[1].append(hs.reshape(nbp, SSD_HEADS, SSD_HEAD_DIM, SSD_STATE))
        s_states[0].append(jnp.concatenate([sx, sb, scc], axis=-1))
        s_states[1].append(shs.reshape(nbs, SSD_HEADS, SSD_HEAD_DIM, SSD_STATE))

        yp, c1, n1, m1 = _mlstm(hid, z_c, z_n, z_m, l, 0, ml_gb, ml_nrm,
                                row0=0, nb=nbp, seq_len=lp, cs=pc, nseq=2, rows=2 * pc)
        ysm, sc1, sn1, sm1 = _mlstm(hid, state_mlstm_c, s_n, s_m, l, l, ml_gb, ml_nrm,
                                    row0=mp, nb=nbs, seq_len=ls, cs=sc, nseq=8, rows=8 * sc)
        y_ml = jnp.concatenate([yp, ysm], axis=0)
        p_states[2].append(c1)
        p_states[3].append(n1.reshape(nbp, ML_HEADS, ML_DK))
        p_states[4].append(m1.reshape(nbp, ML_HEADS))
        s_states[2].append(sc1)
        s_states[3].append(sn1.reshape(nbs, ML_HEADS, ML_DK))
        s_states[4].append(sm1.reshape(nbs, ML_HEADS))

        yp, h1 = _hgrn(hid, z_hg, l, 0, hg_lb_logits, hg_nrm,
                       row0=0, nb=nbp, seq_len=lp, cs=pc, sub=32, nseq=4, rows=4 * pc)
        ysm, sh1 = _hgrn(hid, state_hgrn, l, l, hg_lb_logits, hg_nrm,
                         row0=mp, nb=nbs, seq_len=ls, cs=sc, sub=sc, nseq=32, rows=32 * sc)
        y_hg = jnp.concatenate([yp, ysm], axis=0)
        p_states[5].append(h1)
        s_states[5].append(sh1)

        s_mix = _merge((y_ssd, y_ml, y_hg), hid, wb, l)
        x = _res_ln(x, s_mix, w_mo, lng, lnb, l, 1)

        kv = _matmul(mem_rows, w_kv, l, f32, tm=nbp * MEM_LEN, name="mem_kv")
        pk = kv[:, :D_MODEL].reshape(nbp, MEM_LEN, D_MODEL)
        pv = kv[:, D_MODEL:].reshape(nbp, MEM_LEN, D_MODEL)
        p_states[6].append(pk.reshape(nbp, MEM_LEN, X_HEADS, X_HEAD_DIM))
        p_states[7].append(pv.reshape(nbp, MEM_LEN, X_HEADS, X_HEAD_DIM))

        q = _matmul(x, w_q, l, f32, tm=1024, name="q_proj")
        op = _attention(q, pk, pv, l, 0, nbp, lp, nseq=1, rows=512)
        osm = _attention(q, cache_k, cache_v, l, mp, nbs, ls, nseq=4, rows=ls)
        o = jnp.concatenate([op, osm], axis=0)
        x = _res_ln(x, o, w_o, lng, lnb, l, 2)
        x = _ffn_ln(x, w1p, w3p, w2p, lng, lnb, l, 1, 3)

    y_prompt = x[:mp].reshape(nbp, lp, D_MODEL)
    y_sample = x[mp:].reshape(nbs, ls, D_MODEL)
    return (y_prompt, y_sample, *[jnp.stack(s) for s in p_states], *[jnp.stack(s) for s in s_states])
```

```python
import functools
import math

import jax
import jax.numpy as jnp
from jax import lax
from jax.experimental import pallas as pl
from jax.experimental.pallas import tpu as pltpu

f32 = jnp.float32
bf16 = jnp.bfloat16

D_MODEL = 2048
DEPTH = 2
MIX_W = D_MODEL
SSD_HEAD_DIM = 64
SSD_HEADS = 32
SSD_GROUPS = 4
SSD_STATE = 128
SSD_CONV = 4
SSD_GROUP_W = MIX_W // SSD_GROUPS
SSD_HEADS_PER_GROUP = SSD_HEADS // SSD_GROUPS
ML_HEADS = 4
ML_DV = 512
ML_DK = 256
HG_HEADS = 16
HG_DK = 128
HG_DV = 128
MEM_LEN = 256
X_HEADS = 4
X_HEAD_DIM = 512
D_FF = 5504
D_FF_PAD = 5632
FF_TILE = 512
DN_ALPHA = (2.0 * DEPTH) ** 0.25
LN_EPS = 1e-5
RMS_EPS = 1e-6

LANES = 128
VMEM_LIMIT = 56 * 1024 * 1024

COL_Z = 0
COL_XS = 2048
COL_MV = 4096
COL_MO = 6144
COL_HQ = 8192
COL_HF = 10240
COL_HI = 12288
COL_HG = 14336
COL_GATE = 16384
COL_MQ = 22528
COL_MK = 23552
COL_BM = 24576
COL_CM = 25088
COL_DT = 25600
COL_MG = 26112
N_HID = 26624


def _cp(*sem):
    return pltpu.CompilerParams(dimension_semantics=sem, vmem_limit_bytes=VMEM_LIMIT)


def _dot(a, b):
    return jnp.dot(a, b, preferred_element_type=f32)


def _dot_nt(a, b):
    return lax.dot_general(a, b, (((1,), (1,)), ((), ())), preferred_element_type=f32)


def _dot_tn(a, b):
    return lax.dot_general(a, b, (((0,), (0,)), ((), ())), preferred_element_type=f32)


def _dot_exact(a, b):
    return jnp.dot(a, b, precision=lax.Precision.HIGHEST, preferred_element_type=f32)


def _sigmoid(x):
    return 1.0 / (1.0 + jnp.exp(-x))


def _silu(x):
    return x * _sigmoid(x)


def _softplus(x):
    return jnp.maximum(x, 0.0) + jnp.log1p(jnp.exp(-jnp.abs(x)))


def _log_sigmoid(x):
    return jnp.minimum(x, 0.0) - jnp.log1p(jnp.exp(-jnp.abs(x)))


def _layer_norm(y, g, b):
    mu = jnp.mean(y, axis=-1, keepdims=True)
    yc = y - mu
    var = jnp.mean(yc * yc, axis=-1, keepdims=True)
    return yc * lax.rsqrt(var + LN_EPS) * g + b


def _rms_norm(y, g):
    return y * lax.rsqrt(jnp.mean(y * y, axis=-1, keepdims=True) + RMS_EPS) * g


def _row_tile(m, preferred):
    t = preferred
    while m % t:
        t //= 2
    return t


def _tril(n):
    r = lax.broadcasted_iota(jnp.int32, (n, n), 0)
    c = lax.broadcasted_iota(jnp.int32, (n, n), 1)
    return r >= c


def _ffn_kernel(x_ref, w1_ref, w3_ref, w2_ref, g_ref, b_ref, o_ref, xb_ref, acc_ref):
    j = pl.program_id(1)

    @pl.when(j == 0)
    def _():
        xb_ref[...] = x_ref[...].astype(bf16)
        acc_ref[...] = jnp.zeros_like(acc_ref)

    xb = xb_ref[...]
    a = _dot(xb, w1_ref[...])
    b = _dot(xb, w3_ref[...])
    h = (_silu(a) * b).astype(bf16)
    acc_ref[...] += _dot(h, w2_ref[...])

    @pl.when(j == pl.num_programs(1) - 1)
    def _():
        y = DN_ALPHA * x_ref[...] + 0.5 * acc_ref[...]
        o_ref[...] = _layer_norm(y, g_ref[...], b_ref[...])


def _ffn_ln(x, w1p, w3p, w2p, ln_g, ln_b, l, k, ln_i, tm=512):
    m = x.shape[0]
    tm = _row_tile(m, tm)
    tf = FF_TILE
    return pl.pallas_call(
        _ffn_kernel,
        out_shape=jax.ShapeDtypeStruct((m, D_MODEL), f32),
        grid=(m // tm, D_FF_PAD // tf),
        in_specs=[
            pl.BlockSpec((tm, D_MODEL), lambda i, j: (i, 0)),
            pl.BlockSpec((None, None, D_MODEL, tf), lambda i, j: (l, k, 0, j)),
            pl.BlockSpec((None, None, D_MODEL, tf), lambda i, j: (l, k, 0, j)),
            pl.BlockSpec((None, None, tf, D_MODEL), lambda i, j: (l, k, j, 0)),
            pl.BlockSpec((None, None, 1, D_MODEL), lambda i, j: (l, ln_i, 0, 0)),
            pl.BlockSpec((None, None, 1, D_MODEL), lambda i, j: (l, ln_i, 0, 0)),
        ],
        out_specs=pl.BlockSpec((tm, D_MODEL), lambda i, j: (i, 0)),
        scratch_shapes=[pltpu.VMEM((tm, D_MODEL), bf16), pltpu.VMEM((tm, D_MODEL), f32)],
        compiler_params=_cp("parallel", "arbitrary"),
        name="ffn_ln",
    )(x, w1p, w3p, w2p, ln_g, ln_b)


def _mm_kernel(x_ref, w_ref, o_ref, xb_ref):
    @pl.when(pl.program_id(1) == 0)
    def _():
        xb_ref[...] = x_ref[...].astype(bf16)

    o_ref[...] = _dot(xb_ref[...], w_ref[...]).astype(o_ref.dtype)


def _matmul(x, w, l, out_dtype, tm, tn=512, name="matmul"):
    m, kdim = x.shape
    n = w.shape[-1]
    tm = _row_tile(m, tm)
    return pl.pallas_call(
        _mm_kernel,
        out_shape=jax.ShapeDtypeStruct((m, n), out_dtype),
        grid=(m // tm, n // tn),
        in_specs=[
            pl.BlockSpec((tm, kdim), lambda i, j: (i, 0)),
            pl.BlockSpec((None, kdim, tn), lambda i, j: (l, 0, j)),
        ],
        out_specs=pl.BlockSpec((tm, tn), lambda i, j: (i, j)),
        scratch_shapes=[pltpu.VMEM((tm, kdim), bf16)],
        compiler_params=_cp("parallel", "arbitrary"),
        name=name,
    )(x, w)


def _res_ln_kernel(x_ref, y_ref, w_ref, g_ref, b_ref, o_ref):
    r = _dot(y_ref[...].astype(bf16), w_ref[...])
    o_ref[...] = _layer_norm(DN_ALPHA * x_ref[...] + r, g_ref[...], b_ref[...])


def _res_ln(x, y, w, ln_g, ln_b, l, ln_i, tm=512):
    m = x.shape[0]
    tm = _row_tile(m, tm)
    return pl.pallas_call(
        _res_ln_kernel,
        out_shape=jax.ShapeDtypeStruct((m, D_MODEL), f32),
        grid=(m // tm,),
        in_specs=[
            pl.BlockSpec((tm, D_MODEL), lambda i: (i, 0)),
            pl.BlockSpec((tm, D_MODEL), lambda i: (i, 0)),
            pl.BlockSpec((None, D_MODEL, D_MODEL), lambda i: (l, 0, 0)),
            pl.BlockSpec((None, None, 1, D_MODEL), lambda i: (l, ln_i, 0, 0)),
            pl.BlockSpec((None, None, 1, D_MODEL), lambda i: (l, ln_i, 0, 0)),
        ],
        out_specs=pl.BlockSpec((tm, D_MODEL), lambda i: (i, 0)),
        compiler_params=_cp("parallel"),
        name="res_ln",
    )(x, y, w, ln_g, ln_b)


def _merge_kernel(y0_ref, y1_ref, y2_ref, g0_ref, g1_ref, g2_ref, w_ref, o_ref):
    acc = _sigmoid(g0_ref[...]) * _dot(y0_ref[...].astype(bf16), w_ref[0])
    acc += _sigmoid(g1_ref[...]) * _dot(y1_ref[...].astype(bf16), w_ref[1])
    acc += _sigmoid(g2_ref[...]) * _dot(y2_ref[...].astype(bf16), w_ref[2])
    o_ref[...] = acc.astype(o_ref.dtype)


def _merge(ys, hid, wb, l, tm=512, tn=512):
    m = hid.shape[0]
    tm = _row_tile(m, tm)
    gb = COL_GATE // tn
    nb = D_MODEL // tn
    y_spec = pl.BlockSpec((tm, MIX_W), lambda i, j: (i, 0))
    return pl.pallas_call(
        _merge_kernel,
        out_shape=jax.ShapeDtypeStruct((m, D_MODEL), bf16),
        grid=(m // tm, D_MODEL // tn),
        in_specs=[
            y_spec, y_spec, y_spec,
            pl.BlockSpec((tm, tn), lambda i, j: (i, gb + j)),
            pl.BlockSpec((tm, tn), lambda i, j: (i, gb + nb + j)),
            pl.BlockSpec((tm, tn), lambda i, j: (i, gb + 2 * nb + j)),
            pl.BlockSpec((None, 3, MIX_W, tn), lambda i, j: (l, 0, 0, j)),
        ],
        out_specs=pl.BlockSpec((tm, tn), lambda i, j: (i, j)),
        compiler_params=_cp("parallel", "arbitrary"),
        name="merge",
    )(ys[0], ys[1], ys[2], hid, hid, hid, wb)


def _attn_kernel(q_ref, k_ref, v_ref, o_ref, *, nseq, rows, interleaved):
    scale = X_HEAD_DIM ** -0.5
    nct = X_HEAD_DIM // LANES

    def seq_body(g, carry):
        rs = pl.ds(pl.multiple_of(g * rows, 8), rows)
        for h in range(X_HEADS):
            cs = slice(h * X_HEAD_DIM, (h + 1) * X_HEAD_DIM)
            q = q_ref[rs, cs].astype(bf16)
            if interleaved:
                def tile(ref, c):
                    return ref[g, pl.ds(c * X_HEADS + h, MEM_LEN, stride=X_HEADS * nct), :].astype(bf16)
                s = _dot_nt(q[:, 0:LANES], tile(k_ref, 0))
                for c in range(1, nct):
                    s = s + _dot_nt(q[:, c * LANES:(c + 1) * LANES], tile(k_ref, c))
            else:
                s = _dot_nt(q, k_ref[g, :, cs].astype(bf16))
            s = s * scale
            s = s - jnp.max(s, axis=-1, keepdims=True)
            e = jnp.exp(s)
            p = (e / jnp.sum(e, axis=-1, keepdims=True)).astype(bf16)
            if interleaved:
                for c in range(nct):
                    o_ref[rs, h * X_HEAD_DIM + c * LANES:h * X_HEAD_DIM + (c + 1) * LANES] = _dot(p, tile(v_ref, c))
            else:
                o_ref[rs, cs] = _dot(p, v_ref[g, :, cs].astype(bf16))
        return carry

    lax.fori_loop(0, nseq, seq_body, 0)


def _attention(q, mk, mv, l, row0, nb, seq_len, nseq, rows):
    r = nseq * rows
    steps_per_seq = seq_len // rows if nseq == 1 else 1
    rb0 = row0 // r
    interleaved = mk.ndim == 4
    if interleaved:
        kv_spec = pl.BlockSpec((None, nseq, mk.shape[2], LANES), lambda b, t: (l, b, 0, 0))
    else:
        kv_spec = pl.BlockSpec((nseq, MEM_LEN, D_MODEL), lambda b, t: (b, 0, 0))
    return pl.pallas_call(
        functools.partial(_attn_kernel, nseq=nseq, rows=rows, interleaved=interleaved),
        out_shape=jax.ShapeDtypeStruct((nb * seq_len, D_MODEL), f32),
        grid=(nb // nseq, steps_per_seq),
        in_specs=[
            pl.BlockSpec((r, D_MODEL), lambda b, t: (rb0 + b * steps_per_seq + t, 0)),
            kv_spec, kv_spec,
        ],
        out_specs=pl.BlockSpec((r, D_MODEL), lambda b, t: (b * steps_per_seq + t, 0)),
        compiler_params=_cp("parallel", "arbitrary"),
        name="mem_attn",
    )(q, mk, mv)


def _ssd_kernel(*refs, cs, nseq, per_seq_state, has_prev):
    (z_ref, xs_ref, bm_ref, cm_ref, dt_ref, cx0_ref, cb0_ref, cc0_ref, h0_ref,
     wx_ref, wb_ref, wc_ref, bx_ref, bb_ref, bc_ref, hp_ref, nrm_ref) = refs[:17]
    y_ref, cxo_ref, cbo_ref, cco_ref, ho_ref, upx_ref, upb_ref, upc_ref, yg_ref = refs[17 + has_prev:]
    t_idx = pl.program_id(2)

    @pl.when(t_idx == 0)
    def _():
        ho_ref[...] = h0_ref[...]
        cxo_ref[...] = cx0_ref[...]
        cbo_ref[...] = cb0_ref[...]
        cco_ref[...] = cc0_ref[...]

    tri = _tril(cs)
    tril_f = tri.astype(f32)
    lane = lax.broadcasted_iota(jnp.int32, (cs, LANES), 1)
    left = lane < SSD_HEAD_DIM
    srow = lax.broadcasted_iota(jnp.int32, (LANES, LANES), 0)
    top = srow < SSD_HEAD_DIM
    hp = hp_ref[...]
    a_neg = -jnp.exp(hp[1:2, :])
    gw = SSD_GROUP_W

    def conv(up_ref, st_ref, u_ref, w_ref, b_ref, slot, rs):
        up_ref[5:8, :] = st_ref[slot]
        up_ref[8:8 + cs, :] = u_ref[rs, :]
        out = b_ref[...] + up_ref[5:5 + cs, :] * w_ref[0:1, :]
        for j in range(1, SSD_CONV):
            out = out + up_ref[5 + j:5 + j + cs, :] * w_ref[j:j + 1, :]
        st_ref[slot] = up_ref[5 + cs:8 + cs, :]
        return _silu(out)

    def seq_body(i, carry):
        slot = i if per_seq_state else 0
        rs = pl.ds(pl.multiple_of(i * cs, 8), cs)
        xc = conv(upx_ref, cxo_ref, xs_ref, wx_ref, bx_ref, slot, rs)
        bmat = conv(upb_ref, cbo_ref, bm_ref, wb_ref, bb_ref, slot, rs)
        cmat = conv(upc_ref, cco_ref, cm_ref, wc_ref, bc_ref, slot, rs)
        bmb = bmat.astype(bf16)
        cmb = cmat.astype(bf16)

        dt = _softplus(dt_ref[rs, :] + hp[0:1, :])
        a = dt * a_neg
        acum = _dot_exact(tril_f, a)
        acum_t = acum.T
        dt_t = dt.T
        a_last = acum[cs - 1:cs, :]
        ea = jnp.exp(acum)
        tail = jnp.exp(a_last - acum) * dt
        e_last = jnp.exp(a_last)
        cb = _dot_nt(cmb, bmb)

        for pp in range(SSD_HEADS_PER_GROUP // 2):
            j0, j1 = 2 * pp, 2 * pp + 1
            ls = slice(pp * LANES, (pp + 1) * LANES)
            xp = xc[:, ls]
            xpb = xp.astype(bf16)

            def head_w(j):
                seg = acum[:, j:j + 1] - acum_t[j:j + 1, :]
                dec = jnp.exp(jnp.where(tri, seg, -jnp.inf))
                return (cb * dec * dt_t[j:j + 1, :]).astype(bf16)

            zero = jnp.zeros_like(xpb)
            y = _dot(head_w(j0), jnp.where(left, xpb, zero)) + _dot(head_w(j1), jnp.where(left, zero, xpb))
            hstate = ho_ref[slot, pp * LANES:(pp + 1) * LANES, :]
            y = y + _dot_nt(cmb, hstate.astype(bf16)) * jnp.where(left, ea[:, j0:j0 + 1], ea[:, j1:j1 + 1])
            xw = (xp * jnp.where(left, tail[:, j0:j0 + 1], tail[:, j1:j1 + 1])).astype(bf16)
            dec_rows = jnp.where(top, e_last[:, j0:j0 + 1], e_last[:, j1:j1 + 1])
            ho_ref[slot, pp * LANES:(pp + 1) * LANES, :] = hstate * dec_rows + _dot_tn(xw, bmb)
            d_skip = jnp.where(left, hp[2:3, j0:j0 + 1], hp[2:3, j1:j1 + 1])
            y = (y + xp * d_skip) * _silu(z_ref[rs, ls])
            yg_ref[:, ls] = y

        yg = yg_ref[...]
        y_ref[rs, :] = _rms_norm(yg, nrm_ref[...]).astype(y_ref.dtype)
        return carry

    lax.fori_loop(0, nseq, seq_body, 0)


def _ssd(hid, conv_st, h_st, l, st_l, conv_w, conv_b, hp, nrm, prev, *, row0, nb, seq_len, cs, nseq, rows):
    per_seq_state = seq_len == cs
    has_prev = prev is not None
    nt = seq_len // rows if not per_seq_state else 1
    g = nseq if per_seq_state else 1
    rb0 = row0 // rows
    gw = SSD_GROUP_W

    def rowblk(b, t):
        return rb0 + b * nt + t

    def hid_spec(col0, width):
        return pl.BlockSpec((rows, width), lambda b, gi, t: (rowblk(b, t), col0 // width + gi))

    def st_spec(col0, width):
        return pl.BlockSpec((None, g, 3, width), lambda b, gi, t: (st_l, b, 0, col0 // width + gi))

    def w_spec(nrow, col0, width):
        return pl.BlockSpec((None, nrow, width), lambda b, gi, t: (l, 0, col0 // width + gi))

    def sto_spec(width):
        return pl.BlockSpec((g, 3, width), lambda b, gi, t: (b, 0, gi))

    kern = functools.partial(_ssd_kernel, cs=cs, nseq=nseq, per_seq_state=per_seq_state, has_prev=has_prev)
    args = [hid, hid, hid, hid, hid, conv_st, conv_st, conv_st, h_st,
            conv_w, conv_w, conv_w, conv_b, conv_b, conv_b, hp, nrm] + ([prev] if has_prev else [])
    return pl.pallas_call(
        kern,
        out_shape=(jax.ShapeDtypeStruct((nb * seq_len, MIX_W), f32),
                   jax.ShapeDtypeStruct((nb, 3, MIX_W), f32),
                   jax.ShapeDtypeStruct((nb, 3, SSD_GROUPS * SSD_STATE), f32),
                   jax.ShapeDtypeStruct((nb, 3, SSD_GROUPS * SSD_STATE), f32),
                   jax.ShapeDtypeStruct((DEPTH, nb, SSD_GROUPS, gw, SSD_STATE), f32)),
        grid=(nb // g, SSD_GROUPS, nt),
        input_output_aliases=({len(args) - 1: 4} if has_prev else {}),
        in_specs=[
            hid_spec(COL_Z, gw), hid_spec(COL_XS, gw), hid_spec(COL_BM, LANES), hid_spec(COL_CM, LANES),
            hid_spec(COL_DT, LANES),
            st_spec(0, gw), st_spec(MIX_W, LANES), st_spec(MIX_W + SSD_GROUPS * SSD_STATE, LANES),
            pl.BlockSpec((None, g, None, gw, SSD_STATE), lambda b, gi, t: (st_l, b, gi, 0, 0)),
            w_spec(SSD_CONV, 0, gw), w_spec(SSD_CONV, MIX_W, LANES),
            w_spec(SSD_CONV, MIX_W + SSD_GROUPS * SSD_STATE, LANES),
            w_spec(1, 0, gw), w_spec(1, MIX_W, LANES), w_spec(1, MIX_W + SSD_GROUPS * SSD_STATE, LANES),
            pl.BlockSpec((None, None, 8, LANES), lambda b, gi, t: (l, gi, 0, 0)),
            pl.BlockSpec((None, 1, gw), lambda b, gi, t: (l, 0, gi)),
        ] + ([pl.BlockSpec(memory_space=pl.ANY)] if has_prev else []),
        out_specs=(
            pl.BlockSpec((rows, gw), lambda b, gi, t: (b * nt + t, gi)),
            sto_spec(gw), sto_spec(LANES), sto_spec(LANES),
            pl.BlockSpec((None, g, None, gw, SSD_STATE), lambda b, gi, t: (l, b, gi, 0, 0)),
        ),
        scratch_shapes=[pltpu.VMEM((cs + 8, gw), f32), pltpu.VMEM((cs + 8, LANES), f32),
                        pltpu.VMEM((cs + 8, LANES), f32), pltpu.VMEM((cs, gw), f32)],
        compiler_params=_cp("parallel", "parallel", "arbitrary"),
        name="ssd_scan",
    )(*args)


def _mlstm_kernel(*refs, cs, nseq, per_seq_state, has_prev):
    q_ref, k_ref, v_ref, o_ref, gt_ref, c0_ref, n0_ref, m0_ref, gb_ref, nrm_ref = refs[:10]
    y_ref, co_ref, no_ref, mo_ref = refs[10 + has_prev:]
    t_idx = pl.program_id(2)

    @pl.when(t_idx == 0)
    def _():
        co_ref[...] = c0_ref[...]
        no_ref[...] = n0_ref[...]
        mo_ref[...] = m0_ref[...]

    tri = _tril(cs)
    tril_f = tri.astype(f32)
    kscale = ML_DK ** -0.5

    def seq_body(i, carry):
        slot = i if per_seq_state else 0
        rs = pl.ds(pl.multiple_of(i * cs, 8), cs)
        gates = gt_ref[rs, :] + gb_ref[...]
        bfull = _dot_exact(tril_f, _log_sigmoid(gates))
        gates_t = gates.T
        bfull_t = bfull.T
        logi_c = gates[:, 0:1]
        b_c = bfull[:, 1:2]
        logi_r = gates_t[0:1, :]
        b_r = bfull_t[1:2, :]
        m_prev = mo_ref[slot]

        dmat = jnp.where(tri, b_c - b_r + logi_r, -jnp.inf)
        prev = b_c + m_prev
        mt = jnp.maximum(prev, jnp.max(dmat, axis=-1, keepdims=True))
        wts = jnp.exp(dmat - mt)
        sprev = jnp.exp(prev - mt)

        qh = q_ref[rs, :]
        kh = k_ref[rs, :] * kscale
        qb = qh.astype(bf16)
        kb = kh.astype(bf16)
        vb = v_ref[rs, :].astype(bf16)
        cst = co_ref[slot]
        nrow = no_ref[slot]
        qk = _dot_nt(qb, kb) * wts
        num = _dot(qk.astype(bf16), vb) + _dot(qb, cst.astype(bf16)) * sprev
        den = jnp.sum(qk, axis=-1, keepdims=True) + jnp.sum(qh * nrow, axis=-1, keepdims=True) * sprev
        hh = num / jnp.maximum(jnp.abs(den), jnp.exp(-mt))

        m_new = mt[cs - 1:cs, :]
        b_last = b_c[cs - 1:cs, :]
        dec_prev = jnp.exp(b_last + m_prev - m_new)
        wk = jnp.exp(b_last - b_c + logi_c - m_new)
        kw = kh * wk
        co_ref[slot] = cst * dec_prev + _dot_tn(kw.astype(bf16), vb)
        no_ref[slot] = nrow * dec_prev + jnp.sum(kw, axis=0, keepdims=True)
        mo_ref[slot] = m_new

        y = _rms_norm(hh, nrm_ref[...]) * _sigmoid(o_ref[rs, :])
        y_ref[rs, :] = y.astype(y_ref.dtype)
        return carry

    lax.fori_loop(0, nseq, seq_body, 0)


def _mlstm(hid, c_st, n_st, m_st, l, st_l, gbias, nrm, prev, *, row0, nb, seq_len, cs, nseq, rows):
    per_seq_state = seq_len == cs
    has_prev = prev is not None
    nt = seq_len // rows if not per_seq_state else 1
    g = nseq if per_seq_state else 1
    rb0 = row0 // rows

    def hid_spec(col0, width):
        return pl.BlockSpec((rows, width), lambda b, h, t: (rb0 + b * nt + t, col0 // width + h))

    kern = functools.partial(_mlstm_kernel, cs=cs, nseq=nseq, per_seq_state=per_seq_state, has_prev=has_prev)
    args = [hid, hid, hid, hid, hid, c_st, n_st, m_st, gbias, nrm] + ([prev] if has_prev else [])
    return pl.pallas_call(
        kern,
        input_output_aliases=({len(args) - 1: 1} if has_prev else {}),
        out_shape=(jax.ShapeDtypeStruct((nb * seq_len, MIX_W), f32),
                   jax.ShapeDtypeStruct((DEPTH, nb, ML_HEADS, ML_DK, ML_DV), f32),
                   jax.ShapeDtypeStruct((nb, ML_HEADS, 1, ML_DK), f32),
                   jax.ShapeDtypeStruct((nb, ML_HEADS, 1, 1), f32)),
        grid=(nb // g, ML_HEADS, nt),
        in_specs=[
            hid_spec(COL_MQ, ML_DK), hid_spec(COL_MK, ML_DK), hid_spec(COL_MV, ML_DV), hid_spec(COL_MO, ML_DV),
            hid_spec(COL_MG, LANES),
            pl.BlockSpec((None, g, None, ML_DK, ML_DV), lambda b, h, t: (st_l, b, h, 0, 0)),
            pl.BlockSpec((None, g, None, 1, ML_DK), lambda b, h, t: (st_l, b, h, 0, 0)),
            pl.BlockSpec((None, g, None, 1, 1), lambda b, h, t: (st_l, b, h, 0, 0)),
            pl.BlockSpec((None, None, 1, LANES), lambda b, h, t: (l, h, 0, 0)),
            pl.BlockSpec((None, 1, ML_DV), lambda b, h, t: (l, 0, h)),
        ] + ([pl.BlockSpec(memory_space=pl.ANY)] if has_prev else []),
        out_specs=(
            pl.BlockSpec((rows, ML_DV), lambda b, h, t: (b * nt + t, h)),
            pl.BlockSpec((None, g, None, ML_DK, ML_DV), lambda b, h, t: (l, b, h, 0, 0)),
            pl.BlockSpec((g, None, 1, ML_DK), lambda b, h, t: (b, h, 0, 0)),
            pl.BlockSpec((g, None, 1, 1), lambda b, h, t: (b, h, 0, 0)),
        ),
        compiler_params=_cp("parallel", "parallel", "arbitrary"),
        name="mlstm_scan",
    )(*args)


HG_ROWS = 128
HG_TILE = 8


def _cumsum_rows(tri_b, x):
    hi = x.astype(bf16)
    r1 = x - hi.astype(f32)
    mid = r1.astype(bf16)
    lo = (r1 - mid.astype(f32)).astype(bf16)
    return _dot(tri_b, hi) + _dot(tri_b, mid) + _dot(tri_b, lo)


def _hgrn_kernel(*refs, cs, nblk, per_seq_state, layer, has_prev):
    q_ref, f_ref, i_ref, g_ref, s0_ref, lg_ref, nrm_ref = refs[:7]
    y_ref, so_ref, b_ref, qe_ref, kd_ref, v_ref, yi_ref = refs[7 + has_prev:]
    rb = HG_ROWS
    spb = rb // cs

    @pl.when(pl.program_id(2) == 0)
    def _():
        so_ref[...] = s0_ref[...]

    lg = lg_ref[...]
    pe = jnp.exp(lg - jnp.max(lg, axis=0, keepdims=True))
    pw = pe / jnp.sum(pe, axis=0, keepdims=True)
    lb = jnp.zeros((1, HG_DK), f32)
    for j in range(1, layer + 1):
        lb = lb + pw[j:j + 1, :]
    log_lb = jnp.log(lb)
    log_1m = jnp.log1p(-lb)

    row_a = lax.broadcasted_iota(jnp.int32, (rb, rb), 0)
    lane_a = lax.broadcasted_iota(jnp.int32, (rb, rb), 1)
    row_c = lax.broadcasted_iota(jnp.int32, (rb, HG_DK), 0)
    tri_b = ((row_a >= lane_a) & ((row_a ^ lane_a) < cs)).astype(bf16)

    def rows_of(x, first, width):
        parts = [jnp.broadcast_to(x[j * width + first:j * width + first + 1, :], (width, x.shape[1]))
                 for j in range(rb // width)]
        return parts[0] if len(parts) == 1 else jnp.concatenate(parts, axis=0)

    def block_body(blk, carry):
        rs = pl.ds(pl.multiple_of(blk * rb, rb), rb)
        fr = f_ref[rs, :]
        u = log_1m + _log_sigmoid(fr)
        mx = jnp.maximum(log_lb, u)
        mn = jnp.minimum(log_lb, u)
        logf = mx + jnp.log1p(jnp.exp(mn - mx))
        kk = (1.0 - lb) * _sigmoid(-fr)
        qq = _silu(q_ref[rs, :])
        v = i_ref[rs, :]
        b = _cumsum_rows(tri_b, logf)

        att = jnp.where(lane_a == row_a, jnp.sum(qq * kk, axis=-1, keepdims=True), 0.0)
        for r in range(1, HG_TILE):
            valid = (row_c & (HG_TILE - 1)) >= r
            e = jnp.exp(jnp.where(valid, b - pltpu.roll(b, r, axis=0), 0.0))
            col = jnp.sum(jnp.where(valid, qq * e * pltpu.roll(kk, r, axis=0), 0.0), axis=-1, keepdims=True)
            att = att + jnp.where(lane_a == row_a - r, col, 0.0)
        size = cs
        while size > HG_TILE:
            half = size // 2
            e = jnp.exp(-jnp.abs(b - rows_of(b, half - 1, size)))
            second = (row_c & half) != 0
            q_t = jnp.where(second, qq * e, 0.0).astype(bf16)
            k_t = jnp.where(second, 0.0, kk * e).astype(bf16)
            att = att + jnp.where((row_a ^ lane_a) < size, _dot_nt(q_t, k_t), 0.0)
            size = half

        yi_ref[...] = _dot(att.astype(bf16), v.astype(bf16))
        b_ref[...] = b
        qe_ref[...] = qq * jnp.exp(b)
        kd_ref[...] = kk * jnp.exp(rows_of(b, cs - 1, cs) - b)
        v_ref[...] = v

        def seq_body(j, c):
            slot = blk * spb + j if per_seq_state else 0
            r = pl.ds(pl.multiple_of(j * cs, 8), cs)
            st = so_ref[slot]
            yi_ref[r, :] += _dot(qe_ref[r, :].astype(bf16), st.astype(bf16))
            b_last = b_ref[pl.ds(j * cs + cs - 1, 1), :]
            dcol = jnp.exp(jnp.broadcast_to(b_last, (8, HG_DK))).T[:, 0:1]
            so_ref[slot] = st * dcol + _dot_tn(kd_ref[r, :].astype(bf16), v_ref[r, :].astype(bf16))
            return c

        lax.fori_loop(0, spb, seq_body, 0, unroll=min(spb, 2))
        y = _rms_norm(yi_ref[...], nrm_ref[...]) * _sigmoid(g_ref[rs, :])
        y_ref[rs, :] = y.astype(y_ref.dtype)
        return carry

    lax.fori_loop(0, nblk, block_body, 0)


def _hgrn(hid, s_st, l, st_l, logits, nrm, prev, *, row0, nb, seq_len, cs, nblk):
    rows = nblk * HG_ROWS
    per_seq_state = seq_len == cs
    g = rows // seq_len if per_seq_state else 1
    nt = 1 if per_seq_state else seq_len // rows
    rb0 = row0 // rows
    has_prev = prev is not None

    def hid_spec(col0):
        return pl.BlockSpec((rows, LANES), lambda b, h, t: (rb0 + b * nt + t, col0 // LANES + h))

    kern = functools.partial(_hgrn_kernel, cs=cs, nblk=nblk, per_seq_state=per_seq_state, layer=l, has_prev=has_prev)
    in_specs = [
        hid_spec(COL_HQ), hid_spec(COL_HF), hid_spec(COL_HI), hid_spec(COL_HG),
        pl.BlockSpec((None, g, None, HG_DK, HG_DV), lambda b, h, t: (st_l, b, h, 0, 0)),
        pl.BlockSpec((DEPTH, LANES), lambda b, h, t: (0, h)),
        pl.BlockSpec((None, 1, LANES), lambda b, h, t: (l, 0, h)),
    ]
    args = [hid, hid, hid, hid, s_st, logits, nrm]
    if has_prev:
        in_specs.append(pl.BlockSpec(memory_space=pl.ANY))
        args.append(prev)
    return pl.pallas_call(
        kern,
        out_shape=(jax.ShapeDtypeStruct((nb * seq_len, MIX_W), f32),
                   jax.ShapeDtypeStruct((DEPTH, nb, HG_HEADS, HG_DK, HG_DV), f32)),
        grid=(nb // g, HG_HEADS, nt),
        in_specs=in_specs,
        out_specs=(
            pl.BlockSpec((rows, LANES), lambda b, h, t: (b * nt + t, h)),
            pl.BlockSpec((None, g, None, HG_DK, HG_DV), lambda b, h, t: (l, b, h, 0, 0)),
        ),
        scratch_shapes=[pltpu.VMEM((HG_ROWS, HG_DK), f32)] * 5,
        input_output_aliases=({len(args) - 1: 1} if has_prev else {}),
        compiler_params=_cp("parallel", "parallel", "arbitrary"),
        name="hgrn_scan",
    )(*args)


def _prep_w_in(w_in):
    off = {}
    start = 0
    names = ("z", "xbc", "dt", "mq", "mk", "mv", "mo", "mi", "mf", "hq", "hf", "hi", "hg", "gate")
    sizes = (MIX_W, MIX_W + 2 * SSD_GROUPS * SSD_STATE, SSD_HEADS, ML_HEADS * ML_DK, ML_HEADS * ML_DK, MIX_W, MIX_W,
             ML_HEADS, ML_HEADS, MIX_W, MIX_W, MIX_W, MIX_W, 3 * D_MODEL)
    for nme, sz in zip(names, sizes):
        off[nme] = (start, sz)
        start += sz

    def seg(nme, lo=0, hi=None):
        s, sz = off[nme]
        hi = sz if hi is None else hi
        return w_in[:, :, s + lo:s + hi]

    nbc = SSD_GROUPS * SSD_STATE
    dpth, kdim = w_in.shape[0], w_in.shape[1]
    dt = seg("dt").reshape(dpth, kdim, SSD_GROUPS, SSD_HEADS_PER_GROUP)
    dt = jnp.pad(dt, ((0, 0), (0, 0), (0, 0), (0, LANES - SSD_HEADS_PER_GROUP))).reshape(dpth, kdim, SSD_GROUPS * LANES)
    mg = jnp.stack([seg("mi"), seg("mf")], axis=-1)
    mg = jnp.pad(mg, ((0, 0), (0, 0), (0, 0), (0, LANES - 2))).reshape(dpth, kdim, ML_HEADS * LANES)
    cols = [seg("z"), seg("xbc", 0, MIX_W), seg("mv"), seg("mo"), seg("hq"), seg("hf"), seg("hi"), seg("hg"),
            seg("gate"), seg("mq"), seg("mk"), seg("xbc", MIX_W, MIX_W + nbc), seg("xbc", MIX_W + nbc, MIX_W + 2 * nbc),
            dt, mg]
    return jnp.concatenate(cols, axis=-1).astype(bf16)


def _row_major_groups(x_prompt, x_sample):
    return jnp.concatenate([x_prompt.reshape(-1, D_MODEL), x_sample.reshape(-1, D_MODEL)], axis=0)


def kernel(x_prompt, x_sample, mem_prompt, state_ssd_conv, state_ssd, state_mlstm_c, state_mlstm_n, state_mlstm_m, state_hgrn, cache_mem_k, cache_mem_v, ln_g, ln_b, ffn_w1, ffn_w3, ffn_w2, w_in, ssd_conv_w, ssd_conv_b, ssd_dt_bias, ssd_a_log, ssd_d, ssd_norm, ml_gate_bias, ml_norm, hg_lb_logits, hg_norm, w_branch, w_mix_out, x_wq, x_wk, x_wv, x_wo):
    nbp, lp = x_prompt.shape[0], x_prompt.shape[1]
    nbs, ls = x_sample.shape[0], x_sample.shape[1]
    mp = nbp * lp
    depth = w_in.shape[0]

    ffpad = D_FF_PAD - D_FF
    w1p = jnp.pad(ffn_w1.astype(bf16), ((0, 0), (0, 0), (0, 0), (0, ffpad)))
    w3p = jnp.pad(ffn_w3.astype(bf16), ((0, 0), (0, 0), (0, 0), (0, ffpad)))
    w2p = jnp.pad(ffn_w2.astype(bf16), ((0, 0), (0, 0), (0, ffpad), (0, 0)))
    w_hid = _prep_w_in(w_in)
    wb = w_branch.astype(bf16)
    w_mo = w_mix_out.astype(bf16)
    w_q = x_wq.astype(bf16)
    w_o = x_wo.astype(bf16)
    w_kv = jnp.concatenate([x_wk, x_wv], axis=-1).astype(bf16)
    lng = ln_g.reshape(depth, 4, 1, D_MODEL)
    lnb = ln_b.reshape(depth, 4, 1, D_MODEL)

    def per_head_rows(*rows):
        r = jnp.stack([a.reshape(depth, SSD_GROUPS, SSD_HEADS_PER_GROUP) for a in rows], axis=2)
        return jnp.pad(r, ((0, 0), (0, 0), (0, 8 - len(rows)), (0, LANES - SSD_HEADS_PER_GROUP)))

    ssd_hp = per_head_rows(ssd_dt_bias, ssd_a_log, ssd_d)
    ssd_nrm = ssd_norm.reshape(depth, 1, MIX_W)
    conv_b = ssd_conv_b.reshape(depth, 1, -1)
    ml_gb = jnp.pad(jnp.swapaxes(ml_gate_bias, 1, 2), ((0, 0), (0, 0), (0, LANES - 2))).reshape(depth, ML_HEADS, 1, LANES)
    ml_nrm = ml_norm.reshape(depth, 1, MIX_W)
    hg_nrm = hg_norm.reshape(depth, 1, MIX_W)

    s_ssd = state_ssd.reshape(depth, nbs, SSD_GROUPS, SSD_GROUP_W, SSD_STATE)
    s_n = state_mlstm_n.reshape(depth, nbs, ML_HEADS, 1, ML_DK)
    s_m = state_mlstm_m.reshape(depth, nbs, ML_HEADS, 1, 1)
    nct = X_HEAD_DIM // LANES

    def cache_rows(c):
        c = c.reshape(depth, nbs, MEM_LEN, X_HEADS, nct, LANES)
        return jnp.swapaxes(c, 3, 4).reshape(depth, nbs, MEM_LEN * nct * X_HEADS, LANES)

    cache_k = cache_rows(cache_mem_k)
    cache_v = cache_rows(cache_mem_v)
    z_conv = jnp.zeros((1, nbp, SSD_CONV - 1, state_ssd_conv.shape[-1]), f32)
    z_ssd = jnp.zeros((1, nbp, SSD_GROUPS, SSD_GROUP_W, SSD_STATE), f32)
    z_c = jnp.zeros((1, nbp, ML_HEADS, ML_DK, ML_DV), f32)
    z_n = jnp.zeros((1, nbp, ML_HEADS, 1, ML_DK), f32)
    z_m = jnp.zeros((1, nbp, ML_HEADS, 1, 1), f32)
    z_hg = jnp.zeros((1, nbp, HG_HEADS, HG_DK, HG_DV), f32)

    mem_rows = mem_prompt.reshape(nbp * MEM_LEN, D_MODEL)
    x = _row_major_groups(x_prompt, x_sample)

    p_small = [[] for _ in range(5)]
    s_small = [[] for _ in range(3)]
    p_ssd = p_c = p_hg = s_ssd_o = s_c_o = s_hg_o = None
    pc, sc = 128, ls

    for l in range(depth):
        x = _ffn_ln(x, w1p, w3p, w2p, lng, lnb, l, 0, 0)
        hid = _matmul(x, w_hid, l, f32, tm=1024, name="in_proj")

        yp, cx, cb_, cc, p_ssd = _ssd(hid, z_conv, z_ssd, l, 0, ssd_conv_w, conv_b, ssd_hp, ssd_nrm, p_ssd,
                                      row0=0, nb=nbp, seq_len=lp, cs=pc, nseq=2, rows=2 * pc)
        ysm, sx, sb, scc, s_ssd_o = _ssd(hid, state_ssd_conv, s_ssd, l, l, ssd_conv_w, conv_b, ssd_hp, ssd_nrm, s_ssd_o,
                                         row0=mp, nb=nbs, seq_len=ls, cs=sc, nseq=8, rows=8 * sc)
        y_ssd = jnp.concatenate([yp, ysm], axis=0)
        p_small[0].append(jnp.concatenate([cx, cb_, cc], axis=-1))
        s_small[0].append(jnp.concatenate([sx, sb, scc], axis=-1))

        yp, p_c, n1, m1 = _mlstm(hid, z_c, z_n, z_m, l, 0, ml_gb, ml_nrm, p_c,
                                 row0=0, nb=nbp, seq_len=lp, cs=pc, nseq=2, rows=2 * pc)
        ysm, s_c_o, sn1, sm1 = _mlstm(hid, state_mlstm_c, s_n, s_m, l, l, ml_gb, ml_nrm, s_c_o,
                                      row0=mp, nb=nbs, seq_len=ls, cs=sc, nseq=8, rows=8 * sc)
        y_ml = jnp.concatenate([yp, ysm], axis=0)
        p_small[1].append(n1.reshape(nbp, ML_HEADS, ML_DK))
        p_small[2].append(m1.reshape(nbp, ML_HEADS))
        s_small[1].append(sn1.reshape(nbs, ML_HEADS, ML_DK))
        s_small[2].append(sm1.reshape(nbs, ML_HEADS))

        yp, p_hg = _hgrn(hid, z_hg, l, 0, hg_lb_logits, hg_nrm, p_hg,
                         row0=0, nb=nbp, seq_len=lp, cs=pc, nblk=4)
        ysm, s_hg_o = _hgrn(hid, state_hgrn, l, l, hg_lb_logits, hg_nrm, s_hg_o,
                            row0=mp, nb=nbs, seq_len=ls, cs=sc, nblk=2)
        y_hg = jnp.concatenate([yp, ysm], axis=0)

        s_mix = _merge((y_ssd, y_ml, y_hg), hid, wb, l)
        x = _res_ln(x, s_mix, w_mo, lng, lnb, l, 1)

        kv = _matmul(mem_rows, w_kv, l, f32, tm=nbp * MEM_LEN, name="mem_kv")
        pk = kv[:, :D_MODEL].reshape(nbp, MEM_LEN, D_MODEL)
        pv = kv[:, D_MODEL:].reshape(nbp, MEM_LEN, D_MODEL)
        p_small[3].append(pk.reshape(nbp, MEM_LEN, X_HEADS, X_HEAD_DIM))
        p_small[4].append(pv.reshape(nbp, MEM_LEN, X_HEADS, X_HEAD_DIM))

        q = _matmul(x, w_q, l, f32, tm=1024, name="q_proj")
        op = _attention(q, pk, pv, l, 0, nbp, lp, nseq=1, rows=512)
        osm = _attention(q, cache_k, cache_v, l, mp, nbs, ls, nseq=4, rows=ls)
        o = jnp.concatenate([op, osm], axis=0)
        x = _res_ln(x, o, w_o, lng, lnb, l, 2)
        x = _ffn_ln(x, w1p, w3p, w2p, lng, lnb, l, 1, 3)

    y_prompt = x[:mp].reshape(nbp, lp, D_MODEL)
    y_sample = x[mp:].reshape(nbs, ls, D_MODEL)
    st = jnp.stack
    return (y_prompt, y_sample,
            st(p_small[0]), p_ssd.reshape(depth, nbp, SSD_HEADS, SSD_HEAD_DIM, SSD_STATE), p_c, st(p_small[1]),
            st(p_small[2]), p_hg, st(p_small[3]), st(p_small[4]),
            st(s_small[0]), s_ssd_o.reshape(depth, nbs, SSD_HEADS, SSD_HEAD_DIM, SSD_STATE), s_c_o, st(s_small[1]),
            st(s_small[2]), s_hg_o)
```

```python
import functools

import jax
import jax.numpy as jnp
from jax import lax
from jax.experimental import pallas as pl
from jax.experimental.pallas import tpu as pltpu

f32 = jnp.float32
bf16 = jnp.bfloat16

D_MODEL = 2048
DEPTH = 2
MIX_W = D_MODEL
SSD_HEAD_DIM = 64
SSD_HEADS = 32
SSD_GROUPS = 4
SSD_STATE = 128
SSD_CONV = 4
SSD_GROUP_W = MIX_W // SSD_GROUPS
SSD_HEADS_PER_GROUP = SSD_HEADS // SSD_GROUPS
ML_HEADS = 4
ML_DV = 512
ML_DK = 256
HG_HEADS = 16
HG_DK = 128
HG_DV = 128
MEM_LEN = 256
X_HEADS = 4
X_HEAD_DIM = 512
D_FF = 5504
FF_TILE = 512
FFN_ROWS = 512
DN_ALPHA = (2.0 * DEPTH) ** 0.25
LN_EPS = 1e-5
RMS_EPS = 1e-6

LANES = 128
VMEM_LIMIT = 56 * 1024 * 1024

COL_Z = 0
COL_XS = 2048
COL_MV = 4096
COL_MO = 6144
COL_HQ = 8192
COL_HF = 10240
COL_HI = 12288
COL_HG = 14336
COL_GATE = 16384
COL_MQ = 22528
COL_MK = 23552
COL_BM = 24576
COL_CM = 25088
COL_DT = 25600
COL_MG = 26112
N_HID = 26624


def _cp(*sem):
    return pltpu.CompilerParams(dimension_semantics=sem, vmem_limit_bytes=VMEM_LIMIT)


def _dot(a, b):
    return jnp.dot(a, b, preferred_element_type=f32)


def _dot_nt(a, b):
    return lax.dot_general(a, b, (((1,), (1,)), ((), ())), preferred_element_type=f32)


def _dot_tn(a, b):
    return lax.dot_general(a, b, (((0,), (0,)), ((), ())), preferred_element_type=f32)


def _sigmoid(x):
    return 1.0 / (1.0 + jnp.exp(-x))


def _silu(x):
    return x * _sigmoid(x)


def _softplus(x):
    return jnp.maximum(x, 0.0) + jnp.log1p(jnp.exp(-jnp.abs(x)))


def _log_sigmoid(x):
    return jnp.minimum(x, 0.0) - jnp.log1p(jnp.exp(-jnp.abs(x)))


def _layer_norm(y, g, b):
    mu = jnp.mean(y, axis=-1, keepdims=True)
    yc = y - mu
    var = jnp.mean(yc * yc, axis=-1, keepdims=True)
    return yc * lax.rsqrt(var + LN_EPS) * g + b


def _rms_norm(y, g):
    return y * lax.rsqrt(jnp.mean(y * y, axis=-1, keepdims=True) + RMS_EPS) * g


def _row_tile(m, preferred):
    t = preferred
    while m % t:
        t //= 2
    return t


def _ffn_kernel(x_ref, w1_ref, w3_ref, w2_ref, g_ref, b_ref, o_ref, xb_ref, *, tf, d_ff):
    j = pl.program_id(1)
    last = pl.num_programs(1) - 1

    @pl.when(j == 0)
    def _():
        xb_ref[...] = x_ref[...].astype(bf16)
        o_ref[...] = jnp.zeros_like(o_ref)

    xb = xb_ref[...]
    h = _silu(_dot(xb, w1_ref[...])) * _dot(xb, w3_ref[...])

    if d_ff % tf == 0:
        o_ref[...] += _dot(h.astype(bf16), w2_ref[...])
    else:
        @pl.when(j < last)
        def _():
            o_ref[...] += _dot(h.astype(bf16), w2_ref[...])

        @pl.when(j == last)
        def _():
            valid = d_ff % tf
            w2 = w2_ref[...]
            hm = jnp.where(lax.broadcasted_iota(jnp.int32, (1, tf), 1) < valid, h, 0.0)
            w2m = jnp.where(lax.broadcasted_iota(jnp.int32, (tf, 1), 0) < valid, w2, jnp.zeros_like(w2))
            o_ref[...] += _dot(hm.astype(bf16), w2m)

    @pl.when(j == last)
    def _():
        y = DN_ALPHA * x_ref[...] + 0.5 * o_ref[...]
        o_ref[...] = _layer_norm(y, g_ref[...], b_ref[...])


def _ffn_ln(x, w1, w3, w2, ln_g, ln_b, l, k, ln_i, tm=FFN_ROWS, tf=FF_TILE):
    m = x.shape[0]
    tm = _row_tile(m, tm)
    d_ff = w1.shape[-1]
    return pl.pallas_call(
        functools.partial(_ffn_kernel, tf=tf, d_ff=d_ff),
        out_shape=jax.ShapeDtypeStruct((m, D_MODEL), f32),
        grid=(m // tm, pl.cdiv(d_ff, tf)),
        in_specs=[
            pl.BlockSpec((tm, D_MODEL), lambda i, j: (i, 0)),
            pl.BlockSpec((None, None, D_MODEL, tf), lambda i, j: (l, k, 0, j)),
            pl.BlockSpec((None, None, D_MODEL, tf), lambda i, j: (l, k, 0, j)),
            pl.BlockSpec((None, None, tf, D_MODEL), lambda i, j: (l, k, j, 0)),
            pl.BlockSpec((None, None, 1, D_MODEL), lambda i, j: (l, ln_i, 0, 0)),
            pl.BlockSpec((None, None, 1, D_MODEL), lambda i, j: (l, ln_i, 0, 0)),
        ],
        out_specs=pl.BlockSpec((tm, D_MODEL), lambda i, j: (i, 0)),
        scratch_shapes=[pltpu.VMEM((tm, D_MODEL), bf16)],
        compiler_params=_cp("parallel", "arbitrary"),
        name="ffn_ln",
    )(x, w1, w3, w2, ln_g, ln_b)


def _mm_kernel(x_ref, w_ref, o_ref, xb_ref):
    @pl.when(pl.program_id(1) == 0)
    def _():
        xb_ref[...] = x_ref[...].astype(bf16)

    o_ref[...] = _dot(xb_ref[...], w_ref[...]).astype(o_ref.dtype)


def _matmul(x, w, l, out_dtype, tm, tn=512, name="matmul"):
    m, kdim = x.shape
    n = w.shape[-1]
    tm = _row_tile(m, tm)
    return pl.pallas_call(
        _mm_kernel,
        out_shape=jax.ShapeDtypeStruct((m, n), out_dtype),
        grid=(m // tm, n // tn),
        in_specs=[
            pl.BlockSpec((tm, kdim), lambda i, j: (i, 0)),
            pl.BlockSpec((None, kdim, tn), lambda i, j: (l, 0, j)),
        ],
        out_specs=pl.BlockSpec((tm, tn), lambda i, j: (i, j)),
        scratch_shapes=[pltpu.VMEM((tm, kdim), bf16)],
        compiler_params=_cp("parallel", "arbitrary"),
        name=name,
    )(x, w)


def _res_ln_kernel(x_ref, y_ref, w_ref, g_ref, b_ref, o_ref):
    r = _dot(y_ref[...].astype(bf16), w_ref[...])
    o_ref[...] = _layer_norm(DN_ALPHA * x_ref[...] + r, g_ref[...], b_ref[...])


def _res_ln(x, y, w, ln_g, ln_b, l, ln_i, tm=512):
    m = x.shape[0]
    tm = _row_tile(m, tm)
    return pl.pallas_call(
        _res_ln_kernel,
        out_shape=jax.ShapeDtypeStruct((m, D_MODEL), f32),
        grid=(m // tm,),
        in_specs=[
            pl.BlockSpec((tm, D_MODEL), lambda i: (i, 0)),
            pl.BlockSpec((tm, D_MODEL), lambda i: (i, 0)),
            pl.BlockSpec((None, D_MODEL, D_MODEL), lambda i: (l, 0, 0)),
            pl.BlockSpec((None, None, 1, D_MODEL), lambda i: (l, ln_i, 0, 0)),
            pl.BlockSpec((None, None, 1, D_MODEL), lambda i: (l, ln_i, 0, 0)),
        ],
        out_specs=pl.BlockSpec((tm, D_MODEL), lambda i: (i, 0)),
        compiler_params=_cp("parallel"),
        name="res_ln",
    )(x, y, w, ln_g, ln_b)


def _merge_kernel(y0_ref, y1_ref, y2_ref, g0_ref, g1_ref, g2_ref, w_ref, o_ref):
    acc = _sigmoid(g0_ref[...]) * _dot(y0_ref[...].astype(bf16), w_ref[0])
    acc += _sigmoid(g1_ref[...]) * _dot(y1_ref[...].astype(bf16), w_ref[1])
    acc += _sigmoid(g2_ref[...]) * _dot(y2_ref[...].astype(bf16), w_ref[2])
    o_ref[...] = acc.astype(o_ref.dtype)


def _merge(ys, hid, wb, l, tm=512, tn=512):
    m = hid.shape[0]
    tm = _row_tile(m, tm)
    gb = COL_GATE // tn
    nb = D_MODEL // tn
    y_spec = pl.BlockSpec((tm, MIX_W), lambda i, j: (i, 0))
    return pl.pallas_call(
        _merge_kernel,
        out_shape=jax.ShapeDtypeStruct((m, D_MODEL), bf16),
        grid=(m // tm, D_MODEL // tn),
        in_specs=[
            y_spec, y_spec, y_spec,
            pl.BlockSpec((tm, tn), lambda i, j: (i, gb + j)),
            pl.BlockSpec((tm, tn), lambda i, j: (i, gb + nb + j)),
            pl.BlockSpec((tm, tn), lambda i, j: (i, gb + 2 * nb + j)),
            pl.BlockSpec((None, 3, MIX_W, tn), lambda i, j: (l, 0, 0, j)),
        ],
        out_specs=pl.BlockSpec((tm, tn), lambda i, j: (i, j)),
        compiler_params=_cp("parallel", "arbitrary"),
        name="merge",
    )(ys[0], ys[1], ys[2], hid, hid, hid, wb)


def _attn_kernel(q_ref, k_ref, v_ref, *rest, nseq, rows, interleaved):
    o_ref = rest[-1]
    scale = X_HEAD_DIM ** -0.5
    nct = X_HEAD_DIM // LANES

    def seq_body(g, carry):
        rs = pl.ds(pl.multiple_of(g * rows, 8), rows)
        for h in range(X_HEADS):
            cs = slice(h * X_HEAD_DIM, (h + 1) * X_HEAD_DIM)
            q = q_ref[rs, cs].astype(bf16)
            if interleaved:
                def tile(ref, c):
                    return ref[g, pl.ds(c * X_HEADS + h, MEM_LEN, stride=X_HEADS * nct), :].astype(bf16)
                s = _dot_nt(q[:, 0:LANES], tile(k_ref, 0))
                for c in range(1, nct):
                    s = s + _dot_nt(q[:, c * LANES:(c + 1) * LANES], tile(k_ref, c))
            else:
                s = _dot_nt(q, k_ref[g, :, cs].astype(bf16))
            s = s * scale
            s = s - jnp.max(s, axis=-1, keepdims=True)
            e = jnp.exp(s)
            p = (e / jnp.sum(e, axis=-1, keepdims=True)).astype(bf16)
            if interleaved:
                for c in range(nct):
                    o_ref[rs, h * X_HEAD_DIM + c * LANES:h * X_HEAD_DIM + (c + 1) * LANES] = _dot(p, tile(v_ref, c))
            else:
                o_ref[rs, cs] = _dot(p, v_ref[g, :, cs].astype(bf16))
        return carry

    lax.fori_loop(0, nseq, seq_body, 0)


def _attention(q, mk, mv, l, row0, nb, seq_len, nseq, rows, o_prev=None):
    r = nseq * rows
    steps_per_seq = seq_len // rows if nseq == 1 else 1
    rb0 = row0 // r
    interleaved = mk.ndim == 4
    if interleaved:
        kv_spec = pl.BlockSpec((None, nseq, mk.shape[2], LANES), lambda b, t: (l, b, 0, 0))
    else:
        kv_spec = pl.BlockSpec((nseq, MEM_LEN, D_MODEL), lambda b, t: (b, 0, 0))
    return pl.pallas_call(
        functools.partial(_attn_kernel, nseq=nseq, rows=rows, interleaved=interleaved),
        out_shape=jax.ShapeDtypeStruct(q.shape, f32),
        grid=(nb // nseq, steps_per_seq),
        in_specs=[
            pl.BlockSpec((r, D_MODEL), lambda b, t: (rb0 + b * steps_per_seq + t, 0)),
            kv_spec, kv_spec,
        ] + ([pl.BlockSpec(memory_space=pl.ANY)] if o_prev is not None else []),
        out_specs=pl.BlockSpec((r, D_MODEL), lambda b, t: (rb0 + b * steps_per_seq + t, 0)),
        input_output_aliases=({3: 0} if o_prev is not None else {}),
        compiler_params=_cp("parallel", "arbitrary"),
        name="mem_attn",
    )(*([q, mk, mv] + ([o_prev] if o_prev is not None else [])))


ROW_BLOCK = 128


def _cumsum_rows(tri_b, x):
    hi = x.astype(bf16)
    r1 = x - hi.astype(f32)
    mid = r1.astype(bf16)
    lo = (r1 - mid.astype(f32)).astype(bf16)
    return _dot(tri_b, hi) + _dot(tri_b, mid) + _dot(tri_b, lo)


def _chunk_tril(rb, cs):
    r = lax.broadcasted_iota(jnp.int32, (rb, rb), 0)
    c = lax.broadcasted_iota(jnp.int32, (rb, rb), 1)
    return (r >= c) & ((r ^ c) < cs)


def _rows_of(x, first, width):
    parts = [jnp.broadcast_to(x[j * width + first:j * width + first + 1, :], (width, x.shape[1]))
             for j in range(x.shape[0] // width)]
    return parts[0] if len(parts) == 1 else jnp.concatenate(parts, axis=0)


def _ssd_kernel(*refs, cs, nblk, per_seq_state, n_alias):
    (z_ref, xs_ref, bm_ref, cm_ref, dt_ref, cx0_ref, cb0_ref, cc0_ref, h0_ref,
     wx_ref, wb_ref, wc_ref, bx_ref, bb_ref, bc_ref, hp_ref, nrm_ref) = refs[:17]
    (y_ref, cxo_ref, cbo_ref, cco_ref, ho_ref,
     hx_ref, hb_ref, hc_ref, yg_ref, yi_ref, xw_ref, bs_ref, cs_ref, el_ref) = refs[17 + n_alias:]
    rb = ROW_BLOCK
    spb = rb // cs

    @pl.when(pl.program_id(2) == 0)
    def _():
        ho_ref[...] = h0_ref[...]
        cxo_ref[...] = cx0_ref[...]
        cbo_ref[...] = cb0_ref[...]
        cco_ref[...] = cc0_ref[...]

    hx_ref[...] = jnp.zeros_like(hx_ref)
    hb_ref[...] = jnp.zeros_like(hb_ref)
    hc_ref[...] = jnp.zeros_like(hc_ref)

    tri = _chunk_tril(rb, cs)
    tri_b = tri.astype(bf16)
    lane = lax.broadcasted_iota(jnp.int32, (rb, LANES), 1)
    left = lane < SSD_HEAD_DIM
    top = lax.broadcasted_iota(jnp.int32, (LANES, LANES), 0) < SSD_HEAD_DIM
    hp = hp_ref[...]
    a_neg = -jnp.exp(hp[1:2, :])
    npair = SSD_HEADS_PER_GROUP // 2
    nhist = SSD_CONV - 1

    def block_body(blk, carry):
        r0 = pl.multiple_of(blk * rb, rb)
        rs = pl.ds(r0, rb)

        def slot_of(j):
            return blk * spb + j if per_seq_state else 0

        def conv(h_ref, st_ref, u_ref, w_ref, b_ref):
            u = u_ref[rs, :]
            for j in range(spb):
                h_ref[j * cs:j * cs + nhist, :] = st_ref[slot_of(j)]
            hist = h_ref[...]
            pos = lax.broadcasted_iota(jnp.int32, u.shape, 0) & (cs - 1)
            out = b_ref[...] + u * w_ref[nhist:nhist + 1, :]
            for r in range(1, SSD_CONV):
                up = (rb - (nhist - r)) % rb
                older = jnp.where(pos >= r, pltpu.roll(u, r, axis=0), pltpu.roll(hist, up, axis=0) if up else hist)
                out = out + older * w_ref[nhist - r:nhist - r + 1, :]
            for j in range(spb):
                st_ref[slot_of(j)] = u[j * cs + cs - nhist:j * cs + cs, :]
            return _silu(out)

        xc = conv(hx_ref, cxo_ref, xs_ref, wx_ref, bx_ref)
        bmat = conv(hb_ref, cbo_ref, bm_ref, wb_ref, bb_ref)
        cmat = conv(hc_ref, cco_ref, cm_ref, wc_ref, bc_ref)
        bs_ref[...] = bmat
        cs_ref[...] = cmat
        bmb = bmat.astype(bf16)
        cmb = cmat.astype(bf16)

        dt = _softplus(dt_ref[rs, :] + hp[0:1, :])
        acum = _cumsum_rows(tri_b, dt * a_neg)
        acum_t = acum.T
        dt_t = dt.T
        a_last = _rows_of(acum, cs - 1, cs)
        ea = jnp.exp(acum)
        tail = jnp.exp(a_last - acum) * dt
        el_ref[...] = jnp.exp(a_last)
        cb = _dot_nt(cmb, bmb)

        def pair_lanes(x, pp):
            return jnp.where(left, x[:, 2 * pp:2 * pp + 1], x[:, 2 * pp + 1:2 * pp + 2])

        for pp in range(npair):
            ls = slice(pp * LANES, (pp + 1) * LANES)
            xp = xc[:, ls]
            xpb = xp.astype(bf16)

            def head_w(j):
                seg = acum[:, j:j + 1] - acum_t[j:j + 1, :]
                dec = jnp.exp(jnp.where(tri, seg, -jnp.inf))
                return (cb * dec * dt_t[j:j + 1, :]).astype(bf16)

            zero = jnp.zeros_like(xpb)
            y = _dot(head_w(2 * pp), jnp.where(left, xpb, zero)) + _dot(head_w(2 * pp + 1), jnp.where(left, zero, xpb))
            yg_ref[:, ls] = y + xp * pair_lanes(hp[2:3, :], pp)
            xw_ref[:, ls] = xp * pair_lanes(tail, pp)

        def seq_body(j, c):
            slot = slot_of(j)
            rl = pl.ds(pl.multiple_of(j * cs, 8), cs)
            cj = cs_ref[rl, :].astype(bf16)
            bj = bs_ref[rl, :].astype(bf16)
            el = el_ref[pl.ds(j * cs, 1), :]
            for pp in range(npair):
                ls = slice(pp * LANES, (pp + 1) * LANES)
                hstate = ho_ref[slot, ls, :]
                yi_ref[rl, ls] = _dot_nt(cj, hstate.astype(bf16))
                dec_rows = jnp.where(top, el[:, 2 * pp:2 * pp + 1], el[:, 2 * pp + 1:2 * pp + 2])
                ho_ref[slot, ls, :] = hstate * dec_rows + _dot_tn(xw_ref[rl, ls].astype(bf16), bj)
            return c

        lax.fori_loop(0, spb, seq_body, 0, unroll=min(spb, 2))

        for pp in range(npair):
            ls = slice(pp * LANES, (pp + 1) * LANES)
            yg_ref[:, ls] = (yg_ref[:, ls] + yi_ref[:, ls] * pair_lanes(ea, pp)) * _silu(z_ref[rs, ls])
        y_ref[rs, :] = _rms_norm(yg_ref[...], nrm_ref[...]).astype(y_ref.dtype)
        return carry

    lax.fori_loop(0, nblk, block_body, 0)


def _ssd(hid, conv_st, h_st, l, st_l, conv_w, conv_b, hp, nrm, prev, y_prev, *, row0, nb, seq_len, cs, nblk):
    rows = nblk * ROW_BLOCK
    per_seq_state = seq_len == cs
    g = rows // seq_len if per_seq_state else 1
    nt = 1 if per_seq_state else seq_len // rows
    rb0 = row0 // rows
    gw = SSD_GROUP_W
    extra = [a for a in (prev, y_prev) if a is not None]

    def rowblk(b, t):
        return rb0 + b * nt + t

    def hid_spec(col0, width):
        return pl.BlockSpec((rows, width), lambda b, gi, t: (rowblk(b, t), col0 // width + gi))

    def st_spec(col0, width):
        return pl.BlockSpec((None, g, 3, width), lambda b, gi, t: (st_l, b, 0, col0 // width + gi))

    def w_spec(nrow, col0, width):
        return pl.BlockSpec((None, nrow, width), lambda b, gi, t: (l, 0, col0 // width + gi))

    def sto_spec(width):
        return pl.BlockSpec((g, 3, width), lambda b, gi, t: (b, 0, gi))

    kern = functools.partial(_ssd_kernel, cs=cs, nblk=nblk, per_seq_state=per_seq_state, n_alias=len(extra))
    args = [hid, hid, hid, hid, hid, conv_st, conv_st, conv_st, h_st,
            conv_w, conv_w, conv_w, conv_b, conv_b, conv_b, hp, nrm] + extra
    aliases = {}
    if prev is not None:
        aliases[17] = 4
    if y_prev is not None:
        aliases[len(args) - 1] = 0
    nbc = SSD_GROUPS * SSD_STATE
    rbk = ROW_BLOCK
    return pl.pallas_call(
        kern,
        out_shape=(jax.ShapeDtypeStruct((hid.shape[0], MIX_W), f32),
                   jax.ShapeDtypeStruct((nb, 3, MIX_W), f32),
                   jax.ShapeDtypeStruct((nb, 3, nbc), f32),
                   jax.ShapeDtypeStruct((nb, 3, nbc), f32),
                   jax.ShapeDtypeStruct((DEPTH, nb, SSD_GROUPS, gw, SSD_STATE), f32)),
        grid=(nb // g, SSD_GROUPS, nt),
        input_output_aliases=aliases,
        in_specs=[
            hid_spec(COL_Z, gw), hid_spec(COL_XS, gw), hid_spec(COL_BM, LANES), hid_spec(COL_CM, LANES),
            hid_spec(COL_DT, LANES),
            st_spec(0, gw), st_spec(MIX_W, LANES), st_spec(MIX_W + nbc, LANES),
            pl.BlockSpec((None, g, None, gw, SSD_STATE), lambda b, gi, t: (st_l, b, gi, 0, 0)),
            w_spec(SSD_CONV, 0, gw), w_spec(SSD_CONV, MIX_W, LANES), w_spec(SSD_CONV, MIX_W + nbc, LANES),
            w_spec(1, 0, gw), w_spec(1, MIX_W, LANES), w_spec(1, MIX_W + nbc, LANES),
            pl.BlockSpec((None, None, 8, LANES), lambda b, gi, t: (l, gi, 0, 0)),
            pl.BlockSpec((None, 1, gw), lambda b, gi, t: (l, 0, gi)),
        ] + [pl.BlockSpec(memory_space=pl.ANY)] * len(extra),
        out_specs=(
            pl.BlockSpec((rows, gw), lambda b, gi, t: (rowblk(b, t), gi)),
            sto_spec(gw), sto_spec(LANES), sto_spec(LANES),
            pl.BlockSpec((None, g, None, gw, SSD_STATE), lambda b, gi, t: (l, b, gi, 0, 0)),
        ),
        scratch_shapes=[pltpu.VMEM((rbk, gw), f32), pltpu.VMEM((rbk, LANES), f32), pltpu.VMEM((rbk, LANES), f32),
                        pltpu.VMEM((rbk, gw), f32), pltpu.VMEM((rbk, gw), f32), pltpu.VMEM((rbk, gw), f32),
                        pltpu.VMEM((rbk, LANES), f32), pltpu.VMEM((rbk, LANES), f32), pltpu.VMEM((rbk, LANES), f32)],
        compiler_params=_cp("parallel", "parallel", "arbitrary"),
        name="ssd_scan",
    )(*args)


def _mlstm_kernel(*refs, cs, nblk, per_seq_state, n_alias):
    q_ref, k_ref, v_ref, o_ref, gt_ref, c0_ref, n0_ref, m0_ref, gb_ref, nrm_ref = refs[:10]
    y_ref, co_ref, no_ref, mo_ref, num_ref, kw_ref, dec_ref, mt_ref, sp_ref = refs[10 + n_alias:]
    rb = ROW_BLOCK
    spb = rb // cs

    @pl.when(pl.program_id(2) == 0)
    def _():
        co_ref[...] = c0_ref[...]
        no_ref[...] = n0_ref[...]
        mo_ref[...] = m0_ref[...]

    tri = _chunk_tril(rb, cs)
    tri_b = tri.astype(bf16)
    kscale = ML_DK ** -0.5

    def block_body(blk, carry):
        r0 = pl.multiple_of(blk * rb, rb)
        rs = pl.ds(r0, rb)

        def slot_of(j):
            return blk * spb + j if per_seq_state else 0

        def per_seq(ref, width):
            parts = [jnp.broadcast_to(ref[slot_of(j)], (cs, width)) for j in range(spb)]
            return parts[0] if spb == 1 else jnp.concatenate(parts, axis=0)

        gates = gt_ref[rs, :] + gb_ref[...]
        bfull = _cumsum_rows(tri_b, _log_sigmoid(gates))
        logi_c = gates[:, 0:1]
        b_c = bfull[:, 1:2]
        logi_r = gates.T[0:1, :]
        b_r = bfull.T[1:2, :]
        m_prev = per_seq(mo_ref, 1)
        n_prev = per_seq(no_ref, ML_DK)

        dmat = jnp.where(tri, b_c - b_r + logi_r, -jnp.inf)
        prev = b_c + m_prev
        mt = jnp.maximum(prev, jnp.max(dmat, axis=-1, keepdims=True))
        wts = jnp.exp(dmat - mt)
        sprev = jnp.exp(prev - mt)

        qh = q_ref[rs, :]
        kh = k_ref[rs, :] * kscale
        qk = _dot_nt(qh.astype(bf16), kh.astype(bf16)) * wts
        num_ref[...] = _dot(qk.astype(bf16), v_ref[rs, :].astype(bf16))
        den = jnp.sum(qk, axis=-1, keepdims=True) + jnp.sum(qh * n_prev, axis=-1, keepdims=True) * sprev

        m_new = _rows_of(mt, cs - 1, cs)
        b_last = _rows_of(b_c, cs - 1, cs)
        kw_ref[...] = kh * jnp.exp(b_last - b_c + logi_c - m_new)
        dec_ref[...] = jnp.broadcast_to(jnp.exp(b_last + m_prev - m_new), (rb, LANES))
        mt_ref[...] = jnp.broadcast_to(m_new, (rb, LANES))
        sp_ref[...] = jnp.broadcast_to(sprev, (rb, LANES))

        def seq_body(j, c):
            slot = slot_of(j)
            rl = pl.ds(pl.multiple_of(j * cs, 8), cs)
            rg = pl.ds(pl.multiple_of(r0 + j * cs, 8), cs)
            cst = co_ref[slot]
            num_ref[rl, :] += _dot(q_ref[rg, :].astype(bf16), cst.astype(bf16)) * sp_ref[rl, 0:1]
            kw = kw_ref[rl, :]
            dec = dec_ref[pl.ds(j * cs, 1), 0:1]
            co_ref[slot] = cst * dec + _dot_tn(kw.astype(bf16), v_ref[rg, :].astype(bf16))
            no_ref[slot] = no_ref[slot] * dec + jnp.sum(kw, axis=0, keepdims=True)
            mo_ref[slot] = mt_ref[pl.ds(j * cs, 1), 0:1]
            return c

        lax.fori_loop(0, spb, seq_body, 0, unroll=min(spb, 2))

        hh = num_ref[...] / jnp.maximum(jnp.abs(den), jnp.exp(-mt))
        y = _rms_norm(hh, nrm_ref[...]) * _sigmoid(o_ref[rs, :])
        y_ref[rs, :] = y.astype(y_ref.dtype)
        return carry

    lax.fori_loop(0, nblk, block_body, 0)


def _mlstm(hid, c_st, n_st, m_st, l, st_l, gbias, nrm, prev, y_prev, *, row0, nb, seq_len, cs, nblk):
    rows = nblk * ROW_BLOCK
    per_seq_state = seq_len == cs
    g = rows // seq_len if per_seq_state else 1
    nt = 1 if per_seq_state else seq_len // rows
    rb0 = row0 // rows
    extra = [a for a in (prev, y_prev) if a is not None]

    def hid_spec(col0, width):
        return pl.BlockSpec((rows, width), lambda b, h, t: (rb0 + b * nt + t, col0 // width + h))

    kern = functools.partial(_mlstm_kernel, cs=cs, nblk=nblk, per_seq_state=per_seq_state, n_alias=len(extra))
    args = [hid, hid, hid, hid, hid, c_st, n_st, m_st, gbias, nrm] + extra
    aliases = {}
    if prev is not None:
        aliases[10] = 1
    if y_prev is not None:
        aliases[len(args) - 1] = 0
    rbk = ROW_BLOCK
    return pl.pallas_call(
        kern,
        input_output_aliases=aliases,
        out_shape=(jax.ShapeDtypeStruct((hid.shape[0], MIX_W), f32),
                   jax.ShapeDtypeStruct((DEPTH, nb, ML_HEADS, ML_DK, ML_DV), f32),
                   jax.ShapeDtypeStruct((nb, ML_HEADS, 1, ML_DK), f32),
                   jax.ShapeDtypeStruct((nb, ML_HEADS, 1, 1), f32)),
        grid=(nb // g, ML_HEADS, nt),
        in_specs=[
            hid_spec(COL_MQ, ML_DK), hid_spec(COL_MK, ML_DK), hid_spec(COL_MV, ML_DV), hid_spec(COL_MO, ML_DV),
            hid_spec(COL_MG, LANES),
            pl.BlockSpec((None, g, None, ML_DK, ML_DV), lambda b, h, t: (st_l, b, h, 0, 0)),
            pl.BlockSpec((None, g, None, 1, ML_DK), lambda b, h, t: (st_l, b, h, 0, 0)),
            pl.BlockSpec((None, g, None, 1, 1), lambda b, h, t: (st_l, b, h, 0, 0)),
            pl.BlockSpec((None, None, 1, LANES), lambda b, h, t: (l, h, 0, 0)),
            pl.BlockSpec((None, 1, ML_DV), lambda b, h, t: (l, 0, h)),
        ] + [pl.BlockSpec(memory_space=pl.ANY)] * len(extra),
        out_specs=(
            pl.BlockSpec((rows, ML_DV), lambda b, h, t: (rb0 + b * nt + t, h)),
            pl.BlockSpec((None, g, None, ML_DK, ML_DV), lambda b, h, t: (l, b, h, 0, 0)),
            pl.BlockSpec((g, None, 1, ML_DK), lambda b, h, t: (b, h, 0, 0)),
            pl.BlockSpec((g, None, 1, 1), lambda b, h, t: (b, h, 0, 0)),
        ),
        scratch_shapes=[pltpu.VMEM((rbk, ML_DV), f32), pltpu.VMEM((rbk, ML_DK), f32), pltpu.VMEM((rbk, LANES), f32),
                        pltpu.VMEM((rbk, LANES), f32), pltpu.VMEM((rbk, LANES), f32)],
        compiler_params=_cp("parallel", "parallel", "arbitrary"),
        name="mlstm_scan",
    )(*args)


HG_ROWS = ROW_BLOCK
HG_TILE = 8


def _hgrn_kernel(*refs, cs, nblk, per_seq_state, layer, n_alias):
    q_ref, f_ref, i_ref, g_ref, s0_ref, lg_ref, nrm_ref = refs[:7]
    y_ref, so_ref, b_ref, qe_ref, kd_ref, v_ref, yi_ref = refs[7 + n_alias:]
    rb = HG_ROWS
    spb = rb // cs

    @pl.when(pl.program_id(2) == 0)
    def _():
        so_ref[...] = s0_ref[...]

    lg = lg_ref[...]
    pe = jnp.exp(lg - jnp.max(lg, axis=0, keepdims=True))
    pw = pe / jnp.sum(pe, axis=0, keepdims=True)
    lb = jnp.zeros((1, HG_DK), f32)
    for j in range(1, layer + 1):
        lb = lb + pw[j:j + 1, :]
    log_lb = jnp.log(lb)
    log_1m = jnp.log1p(-lb)

    row_a = lax.broadcasted_iota(jnp.int32, (rb, rb), 0)
    lane_a = lax.broadcasted_iota(jnp.int32, (rb, rb), 1)
    row_c = lax.broadcasted_iota(jnp.int32, (rb, HG_DK), 0)
    tri_b = ((row_a >= lane_a) & ((row_a ^ lane_a) < cs)).astype(bf16)

    def block_body(blk, carry):
        rs = pl.ds(pl.multiple_of(blk * rb, rb), rb)
        fr = f_ref[rs, :]
        u = log_1m + _log_sigmoid(fr)
        mx = jnp.maximum(log_lb, u)
        mn = jnp.minimum(log_lb, u)
        logf = mx + jnp.log1p(jnp.exp(mn - mx))
        kk = (1.0 - lb) * _sigmoid(-fr)
        qq = _silu(q_ref[rs, :])
        v = i_ref[rs, :]
        b = _cumsum_rows(tri_b, logf)

        att = jnp.where(lane_a == row_a, jnp.sum(qq * kk, axis=-1, keepdims=True), 0.0)
        for r in range(1, HG_TILE):
            valid = (row_c & (HG_TILE - 1)) >= r
            e = jnp.exp(jnp.where(valid, b - pltpu.roll(b, r, axis=0), 0.0))
            col = jnp.sum(jnp.where(valid, qq * e * pltpu.roll(kk, r, axis=0), 0.0), axis=-1, keepdims=True)
            att = att + jnp.where(lane_a == row_a - r, col, 0.0)
        size = cs
        while size > HG_TILE:
            half = size // 2
            e = jnp.exp(-jnp.abs(b - _rows_of(b, half - 1, size)))
            second = (row_c & half) != 0
            q_t = jnp.where(second, qq * e, 0.0).astype(bf16)
            k_t = jnp.where(second, 0.0, kk * e).astype(bf16)
            att = att + jnp.where((row_a ^ lane_a) < size, _dot_nt(q_t, k_t), 0.0)
            size = half

        yi_ref[...] = _dot(att.astype(bf16), v.astype(bf16))
        b_ref[...] = b
        qe_ref[...] = qq * jnp.exp(b)
        kd_ref[...] = kk * jnp.exp(_rows_of(b, cs - 1, cs) - b)
        v_ref[...] = v

        def seq_body(j, c):
            slot = blk * spb + j if per_seq_state else 0
            r = pl.ds(pl.multiple_of(j * cs, 8), cs)
            st = so_ref[slot]
            yi_ref[r, :] += _dot(qe_ref[r, :].astype(bf16), st.astype(bf16))
            b_last = b_ref[pl.ds(j * cs + cs - 1, 1), :]
            dcol = jnp.exp(jnp.broadcast_to(b_last, (8, HG_DK))).T[:, 0:1]
            so_ref[slot] = st * dcol + _dot_tn(kd_ref[r, :].astype(bf16), v_ref[r, :].astype(bf16))
            return c

        lax.fori_loop(0, spb, seq_body, 0, unroll=min(spb, 2))
        y = _rms_norm(yi_ref[...], nrm_ref[...]) * _sigmoid(g_ref[rs, :])
        y_ref[rs, :] = y.astype(y_ref.dtype)
        return carry

    lax.fori_loop(0, nblk, block_body, 0)


def _hgrn(hid, s_st, l, st_l, logits, nrm, prev, y_prev, *, row0, nb, seq_len, cs, nblk):
    rows = nblk * HG_ROWS
    per_seq_state = seq_len == cs
    g = rows // seq_len if per_seq_state else 1
    nt = 1 if per_seq_state else seq_len // rows
    rb0 = row0 // rows
    extra = [a for a in (prev, y_prev) if a is not None]

    def hid_spec(col0):
        return pl.BlockSpec((rows, LANES), lambda b, h, t: (rb0 + b * nt + t, col0 // LANES + h))

    kern = functools.partial(_hgrn_kernel, cs=cs, nblk=nblk, per_seq_state=per_seq_state, layer=l, n_alias=len(extra))
    in_specs = [
        hid_spec(COL_HQ), hid_spec(COL_HF), hid_spec(COL_HI), hid_spec(COL_HG),
        pl.BlockSpec((None, g, None, HG_DK, HG_DV), lambda b, h, t: (st_l, b, h, 0, 0)),
        pl.BlockSpec((DEPTH, LANES), lambda b, h, t: (0, h)),
        pl.BlockSpec((None, 1, LANES), lambda b, h, t: (l, 0, h)),
    ]
    args = [hid, hid, hid, hid, s_st, logits, nrm] + extra
    in_specs += [pl.BlockSpec(memory_space=pl.ANY)] * len(extra)
    aliases = {}
    if prev is not None:
        aliases[7] = 1
    if y_prev is not None:
        aliases[len(args) - 1] = 0
    return pl.pallas_call(
        kern,
        out_shape=(jax.ShapeDtypeStruct((hid.shape[0], MIX_W), f32),
                   jax.ShapeDtypeStruct((DEPTH, nb, HG_HEADS, HG_DK, HG_DV), f32)),
        grid=(nb // g, HG_HEADS, nt),
        in_specs=in_specs,
        out_specs=(
            pl.BlockSpec((rows, LANES), lambda b, h, t: (rb0 + b * nt + t, h)),
            pl.BlockSpec((None, g, None, HG_DK, HG_DV), lambda b, h, t: (l, b, h, 0, 0)),
        ),
        scratch_shapes=[pltpu.VMEM((HG_ROWS, HG_DK), f32)] * 5,
        input_output_aliases=aliases,
        compiler_params=_cp("parallel", "parallel", "arbitrary"),
        name="hgrn_scan",
    )(*args)


def _prep_w_in(w_in):
    off = {}
    start = 0
    names = ("z", "xbc", "dt", "mq", "mk", "mv", "mo", "mi", "mf", "hq", "hf", "hi", "hg", "gate")
    sizes = (MIX_W, MIX_W + 2 * SSD_GROUPS * SSD_STATE, SSD_HEADS, ML_HEADS * ML_DK, ML_HEADS * ML_DK, MIX_W, MIX_W,
             ML_HEADS, ML_HEADS, MIX_W, MIX_W, MIX_W, MIX_W, 3 * D_MODEL)
    for nme, sz in zip(names, sizes):
        off[nme] = (start, sz)
        start += sz

    def seg(nme, lo=0, hi=None):
        s, sz = off[nme]
        hi = sz if hi is None else hi
        return w_in[:, :, s + lo:s + hi]

    nbc = SSD_GROUPS * SSD_STATE
    dpth, kdim = w_in.shape[0], w_in.shape[1]
    dt = seg("dt").reshape(dpth, kdim, SSD_GROUPS, SSD_HEADS_PER_GROUP)
    dt = jnp.pad(dt, ((0, 0), (0, 0), (0, 0), (0, LANES - SSD_HEADS_PER_GROUP))).reshape(dpth, kdim, SSD_GROUPS * LANES)
    mg = jnp.stack([seg("mi"), seg("mf")], axis=-1)
    mg = jnp.pad(mg, ((0, 0), (0, 0), (0, 0), (0, LANES - 2))).reshape(dpth, kdim, ML_HEADS * LANES)
    cols = [seg("z"), seg("xbc", 0, MIX_W), seg("mv"), seg("mo"), seg("hq"), seg("hf"), seg("hi"), seg("hg"),
            seg("gate"), seg("mq"), seg("mk"), seg("xbc", MIX_W, MIX_W + nbc), seg("xbc", MIX_W + nbc, MIX_W + 2 * nbc),
            dt, mg]
    return jnp.concatenate(cols, axis=-1).astype(bf16)


def _row_major_groups(x_prompt, x_sample):
    return jnp.concatenate([x_prompt.reshape(-1, D_MODEL), x_sample.reshape(-1, D_MODEL)], axis=0)


def kernel(x_prompt, x_sample, mem_prompt, state_ssd_conv, state_ssd, state_mlstm_c, state_mlstm_n, state_mlstm_m, state_hgrn, cache_mem_k, cache_mem_v, ln_g, ln_b, ffn_w1, ffn_w3, ffn_w2, w_in, ssd_conv_w, ssd_conv_b, ssd_dt_bias, ssd_a_log, ssd_d, ssd_norm, ml_gate_bias, ml_norm, hg_lb_logits, hg_norm, w_branch, w_mix_out, x_wq, x_wk, x_wv, x_wo):
    nbp, lp = x_prompt.shape[0], x_prompt.shape[1]
    nbs, ls = x_sample.shape[0], x_sample.shape[1]
    mp = nbp * lp
    depth = w_in.shape[0]

    w1b = ffn_w1.astype(bf16)
    w3b = ffn_w3.astype(bf16)
    w2b = ffn_w2.astype(bf16)
    w_hid = _prep_w_in(w_in)
    wb = w_branch.astype(bf16)
    w_mo = w_mix_out.astype(bf16)
    w_q = x_wq.astype(bf16)
    w_o = x_wo.astype(bf16)
    w_kv = jnp.concatenate([x_wk, x_wv], axis=-1).astype(bf16)
    lng = ln_g.reshape(depth, 4, 1, D_MODEL)
    lnb = ln_b.reshape(depth, 4, 1, D_MODEL)

    def per_head_rows(*rows):
        r = jnp.stack([a.reshape(depth, SSD_GROUPS, SSD_HEADS_PER_GROUP) for a in rows], axis=2)
        return jnp.pad(r, ((0, 0), (0, 0), (0, 8 - len(rows)), (0, LANES - SSD_HEADS_PER_GROUP)))

    ssd_hp = per_head_rows(ssd_dt_bias, ssd_a_log, ssd_d)
    ssd_nrm = ssd_norm.reshape(depth, 1, MIX_W)
    conv_b = ssd_conv_b.reshape(depth, 1, -1)
    ml_gb = jnp.pad(jnp.swapaxes(ml_gate_bias, 1, 2), ((0, 0), (0, 0), (0, LANES - 2))).reshape(depth, ML_HEADS, 1, LANES)
    ml_nrm = ml_norm.reshape(depth, 1, MIX_W)
    hg_nrm = hg_norm.reshape(depth, 1, MIX_W)

    s_ssd = state_ssd.reshape(depth, nbs, SSD_GROUPS, SSD_GROUP_W, SSD_STATE)
    s_n = state_mlstm_n.reshape(depth, nbs, ML_HEADS, 1, ML_DK)
    s_m = state_mlstm_m.reshape(depth, nbs, ML_HEADS, 1, 1)
    nct = X_HEAD_DIM // LANES

    def cache_rows(c):
        c = c.reshape(depth, nbs, MEM_LEN, X_HEADS, nct, LANES)
        return jnp.swapaxes(c, 3, 4).reshape(depth, nbs, MEM_LEN * nct * X_HEADS, LANES)

    cache_k = cache_rows(cache_mem_k)
    cache_v = cache_rows(cache_mem_v)
    z_conv = jnp.zeros((1, nbp, SSD_CONV - 1, state_ssd_conv.shape[-1]), f32)
    z_ssd = jnp.zeros((1, nbp, SSD_GROUPS, SSD_GROUP_W, SSD_STATE), f32)
    z_c = jnp.zeros((1, nbp, ML_HEADS, ML_DK, ML_DV), f32)
    z_n = jnp.zeros((1, nbp, ML_HEADS, 1, ML_DK), f32)
    z_m = jnp.zeros((1, nbp, ML_HEADS, 1, 1), f32)
    z_hg = jnp.zeros((1, nbp, HG_HEADS, HG_DK, HG_DV), f32)

    mem_rows = mem_prompt.reshape(nbp * MEM_LEN, D_MODEL)
    x = _row_major_groups(x_prompt, x_sample)

    p_small = [[] for _ in range(5)]
    s_small = [[] for _ in range(3)]
    p_ssd = p_c = p_hg = s_ssd_o = s_c_o = s_hg_o = None
    pc, sc = 128, ls

    for l in range(depth):
        x = _ffn_ln(x, w1b, w3b, w2b, lng, lnb, l, 0, 0)
        hid = _matmul(x, w_hid, l, f32, tm=1024, name="in_proj")

        yp, cx, cb_, cc, p_ssd = _ssd(hid, z_conv, z_ssd, l, 0, ssd_conv_w, conv_b, ssd_hp, ssd_nrm, p_ssd, None,
                                      row0=0, nb=nbp, seq_len=lp, cs=pc, nblk=2)
        y_ssd, sx, sb, scc, s_ssd_o = _ssd(hid, state_ssd_conv, s_ssd, l, l, ssd_conv_w, conv_b, ssd_hp, ssd_nrm, s_ssd_o,
                                           yp, row0=mp, nb=nbs, seq_len=ls, cs=sc, nblk=1)
        p_small[0].append(jnp.concatenate([cx, cb_, cc], axis=-1))
        s_small[0].append(jnp.concatenate([sx, sb, scc], axis=-1))

        yp, p_c, n1, m1 = _mlstm(hid, z_c, z_n, z_m, l, 0, ml_gb, ml_nrm, p_c, None,
                                 row0=0, nb=nbp, seq_len=lp, cs=pc, nblk=2)
        y_ml, s_c_o, sn1, sm1 = _mlstm(hid, state_mlstm_c, s_n, s_m, l, l, ml_gb, ml_nrm, s_c_o, yp,
                                       row0=mp, nb=nbs, seq_len=ls, cs=sc, nblk=1)
        p_small[1].append(n1.reshape(nbp, ML_HEADS, ML_DK))
        p_small[2].append(m1.reshape(nbp, ML_HEADS))
        s_small[1].append(sn1.reshape(nbs, ML_HEADS, ML_DK))
        s_small[2].append(sm1.reshape(nbs, ML_HEADS))

        yp, p_hg = _hgrn(hid, z_hg, l, 0, hg_lb_logits, hg_nrm, p_hg, None,
                         row0=0, nb=nbp, seq_len=lp, cs=pc, nblk=4)
        y_hg, s_hg_o = _hgrn(hid, state_hgrn, l, l, hg_lb_logits, hg_nrm, s_hg_o, yp,
                             row0=mp, nb=nbs, seq_len=ls, cs=sc, nblk=2)

        s_mix = _merge((y_ssd, y_ml, y_hg), hid, wb, l)
        x = _res_ln(x, s_mix, w_mo, lng, lnb, l, 1)

        kv = _matmul(mem_rows, w_kv, l, f32, tm=nbp * MEM_LEN, name="mem_kv")
        pk = kv[:, :D_MODEL].reshape(nbp, MEM_LEN, D_MODEL)
        pv = kv[:, D_MODEL:].reshape(nbp, MEM_LEN, D_MODEL)
        p_small[3].append(pk.reshape(nbp, MEM_LEN, X_HEADS, X_HEAD_DIM))
        p_small[4].append(pv.reshape(nbp, MEM_LEN, X_HEADS, X_HEAD_DIM))

        q = _matmul(x, w_q, l, f32, tm=1024, name="q_proj")
        op = _attention(q, pk, pv, l, 0, nbp, lp, nseq=1, rows=512)
        o = _attention(q, cache_k, cache_v, l, mp, nbs, ls, nseq=4, rows=ls, o_prev=op)
        x = _res_ln(x, o, w_o, lng, lnb, l, 2)
        x = _ffn_ln(x, w1b, w3b, w2b, lng, lnb, l, 1, 3)

    y_prompt = x[:mp].reshape(nbp, lp, D_MODEL)
    y_sample = x[mp:].reshape(nbs, ls, D_MODEL)
    st = jnp.stack
    return (y_prompt, y_sample,
            st(p_small[0]), p_ssd.reshape(depth, nbp, SSD_HEADS, SSD_HEAD_DIM, SSD_STATE), p_c, st(p_small[1]),
            st(p_small[2]), p_hg, st(p_small[3]), st(p_small[4]),
            st(s_small[0]), s_ssd_o.reshape(depth, nbs, SSD_HEADS, SSD_HEAD_DIM, SSD_STATE), s_c_o, st(s_small[1]),
            st(s_small[2]), s_hg_o)
```

```python
import functools

import jax
import jax.numpy as jnp
from jax import lax
from jax.experimental import pallas as pl
from jax.experimental.pallas import tpu as pltpu

f32 = jnp.float32
bf16 = jnp.bfloat16

D_MODEL = 2048
DEPTH = 2
MIX_W = D_MODEL
SSD_HEAD_DIM = 64
SSD_HEADS = 32
SSD_GROUPS = 4
SSD_STATE = 128
SSD_CONV = 4
SSD_GROUP_W = MIX_W // SSD_GROUPS
SSD_HEADS_PER_GROUP = SSD_HEADS // SSD_GROUPS
ML_HEADS = 4
ML_DV = 512
ML_DK = 256
HG_HEADS = 16
HG_DK = 128
HG_DV = 128
MEM_LEN = 256
X_HEADS = 4
X_HEAD_DIM = 512
D_FF = 5504
FF_TILE = 512
FFN_ROWS = 1024
DN_ALPHA = (2.0 * DEPTH) ** 0.25
LN_EPS = 1e-5
RMS_EPS = 1e-6

LANES = 128
VMEM_LIMIT = 56 * 1024 * 1024

COL_Z = 0
COL_XS = 2048
COL_MV = 4096
COL_MO = 6144
COL_HQ = 8192
COL_HF = 10240
COL_HI = 12288
COL_HG = 14336
COL_GATE = 16384
COL_MQ = 22528
COL_MK = 23552
COL_BM = 24576
COL_CM = 25088
COL_DT = 25600
COL_MG = 26112
N_HID = 26624


def _cp(*sem):
    return pltpu.CompilerParams(dimension_semantics=sem, vmem_limit_bytes=VMEM_LIMIT)


def _dot(a, b):
    return jnp.dot(a, b, preferred_element_type=f32)


def _dot_nt(a, b):
    return lax.dot_general(a, b, (((1,), (1,)), ((), ())), preferred_element_type=f32)


def _dot_tn(a, b):
    return lax.dot_general(a, b, (((0,), (0,)), ((), ())), preferred_element_type=f32)


def _sigmoid(x):
    return 1.0 / (1.0 + jnp.exp(-x))


def _silu(x):
    return x * _sigmoid(x)


def _softplus(x):
    return jnp.maximum(x, 0.0) + jnp.log1p(jnp.exp(-jnp.abs(x)))


def _log_sigmoid(x):
    return jnp.minimum(x, 0.0) - jnp.log1p(jnp.exp(-jnp.abs(x)))


def _layer_norm(y, g, b):
    mu = jnp.mean(y, axis=-1, keepdims=True)
    yc = y - mu
    var = jnp.mean(yc * yc, axis=-1, keepdims=True)
    return yc * lax.rsqrt(var + LN_EPS) * g + b


def _rms_norm(y, g):
    return y * lax.rsqrt(jnp.mean(y * y, axis=-1, keepdims=True) + RMS_EPS) * g


def _row_tile(m, preferred):
    t = preferred
    while m % t:
        t //= 2
    return t


def _ffn_kernel(x_ref, w1_ref, w3_ref, w2_ref, g_ref, b_ref, o_ref, xb_ref, *, tf, d_ff):
    j = pl.program_id(1)
    last = pl.num_programs(1) - 1

    @pl.when(j == 0)
    def _():
        xb_ref[...] = x_ref[...].astype(bf16)
        o_ref[...] = jnp.zeros_like(o_ref)

    xb = xb_ref[...]
    h = _silu(_dot(xb, w1_ref[...])) * _dot(xb, w3_ref[...])

    if d_ff % tf == 0:
        o_ref[...] += _dot(h.astype(bf16), w2_ref[...])
    else:
        @pl.when(j < last)
        def _():
            o_ref[...] += _dot(h.astype(bf16), w2_ref[...])

        @pl.when(j == last)
        def _():
            valid = d_ff % tf
            w2 = w2_ref[...]
            hm = jnp.where(lax.broadcasted_iota(jnp.int32, (1, tf), 1) < valid, h, 0.0)
            w2m = jnp.where(lax.broadcasted_iota(jnp.int32, (tf, 1), 0) < valid, w2, jnp.zeros_like(w2))
            o_ref[...] += _dot(hm.astype(bf16), w2m)

    @pl.when(j == last)
    def _():
        y = DN_ALPHA * x_ref[...] + 0.5 * o_ref[...]
        o_ref[...] = _layer_norm(y, g_ref[...], b_ref[...])


def _ffn_ln(x, w1, w3, w2, ln_g, ln_b, l, k, ln_i, tm=FFN_ROWS, tf=FF_TILE):
    m = x.shape[0]
    tm = _row_tile(m, tm)
    d_ff = w1.shape[-1]
    return pl.pallas_call(
        functools.partial(_ffn_kernel, tf=tf, d_ff=d_ff),
        out_shape=jax.ShapeDtypeStruct((m, D_MODEL), f32),
        grid=(m // tm, pl.cdiv(d_ff, tf)),
        in_specs=[
            pl.BlockSpec((tm, D_MODEL), lambda i, j: (i, 0), pipeline_mode=pl.Buffered(1)),
            pl.BlockSpec((None, None, D_MODEL, tf), lambda i, j: (l, k, 0, j)),
            pl.BlockSpec((None, None, D_MODEL, tf), lambda i, j: (l, k, 0, j)),
            pl.BlockSpec((None, None, tf, D_MODEL), lambda i, j: (l, k, j, 0)),
            pl.BlockSpec((None, None, 1, D_MODEL), lambda i, j: (l, ln_i, 0, 0)),
            pl.BlockSpec((None, None, 1, D_MODEL), lambda i, j: (l, ln_i, 0, 0)),
        ],
        out_specs=pl.BlockSpec((tm, D_MODEL), lambda i, j: (i, 0)),
        scratch_shapes=[pltpu.VMEM((tm, D_MODEL), bf16)],
        compiler_params=_cp("parallel", "arbitrary"),
        name="ffn_ln",
    )(x, w1, w3, w2, ln_g, ln_b)


def _mm_kernel(x_ref, w_ref, o_ref, xb_ref):
    @pl.when(pl.program_id(1) == 0)
    def _():
        xb_ref[...] = x_ref[...].astype(bf16)

    o_ref[...] = _dot(xb_ref[...], w_ref[...]).astype(o_ref.dtype)


def _matmul(x, w, l, out_dtype, tm, tn=512, name="matmul"):
    m, kdim = x.shape
    n = w.shape[-1]
    tm = _row_tile(m, tm)
    return pl.pallas_call(
        _mm_kernel,
        out_shape=jax.ShapeDtypeStruct((m, n), out_dtype),
        grid=(m // tm, n // tn),
        in_specs=[
            pl.BlockSpec((tm, kdim), lambda i, j: (i, 0)),
            pl.BlockSpec((None, kdim, tn), lambda i, j: (l, 0, j)),
        ],
        out_specs=pl.BlockSpec((tm, tn), lambda i, j: (i, j)),
        scratch_shapes=[pltpu.VMEM((tm, kdim), bf16)],
        compiler_params=_cp("parallel", "arbitrary"),
        name=name,
    )(x, w)


def _res_ln_kernel(x_ref, y_ref, w_ref, g_ref, b_ref, o_ref):
    r = _dot(y_ref[...].astype(bf16), w_ref[...])
    o_ref[...] = _layer_norm(DN_ALPHA * x_ref[...] + r, g_ref[...], b_ref[...])


def _res_ln(x, y, w, ln_g, ln_b, l, ln_i, tm=512):
    m = x.shape[0]
    tm = _row_tile(m, tm)
    return pl.pallas_call(
        _res_ln_kernel,
        out_shape=jax.ShapeDtypeStruct((m, D_MODEL), f32),
        grid=(m // tm,),
        in_specs=[
            pl.BlockSpec((tm, D_MODEL), lambda i: (i, 0)),
            pl.BlockSpec((tm, D_MODEL), lambda i: (i, 0)),
            pl.BlockSpec((None, D_MODEL, D_MODEL), lambda i: (l, 0, 0)),
            pl.BlockSpec((None, None, 1, D_MODEL), lambda i: (l, ln_i, 0, 0)),
            pl.BlockSpec((None, None, 1, D_MODEL), lambda i: (l, ln_i, 0, 0)),
        ],
        out_specs=pl.BlockSpec((tm, D_MODEL), lambda i: (i, 0)),
        compiler_params=_cp("parallel"),
        name="res_ln",
    )(x, y, w, ln_g, ln_b)


def _merge_kernel(y0_ref, y1_ref, y2_ref, g0_ref, g1_ref, g2_ref, w_ref, o_ref, yb_ref):
    @pl.when(pl.program_id(1) == 0)
    def _():
        yb_ref[0] = y0_ref[...].astype(bf16)
        yb_ref[1] = y1_ref[...].astype(bf16)
        yb_ref[2] = y2_ref[...].astype(bf16)

    acc = _sigmoid(g0_ref[...]) * _dot(yb_ref[0], w_ref[0])
    acc += _sigmoid(g1_ref[...]) * _dot(yb_ref[1], w_ref[1])
    acc += _sigmoid(g2_ref[...]) * _dot(yb_ref[2], w_ref[2])
    o_ref[...] = acc.astype(o_ref.dtype)


def _merge(ys, hid, wb, l, tm=512, tn=512):
    m = hid.shape[0]
    tm = _row_tile(m, tm)
    gb = COL_GATE // tn
    nb = D_MODEL // tn
    y_spec = pl.BlockSpec((tm, MIX_W), lambda i, j: (i, 0))
    return pl.pallas_call(
        _merge_kernel,
        out_shape=jax.ShapeDtypeStruct((m, D_MODEL), bf16),
        grid=(m // tm, D_MODEL // tn),
        in_specs=[
            y_spec, y_spec, y_spec,
            pl.BlockSpec((tm, tn), lambda i, j: (i, gb + j)),
            pl.BlockSpec((tm, tn), lambda i, j: (i, gb + nb + j)),
            pl.BlockSpec((tm, tn), lambda i, j: (i, gb + 2 * nb + j)),
            pl.BlockSpec((None, 3, MIX_W, tn), lambda i, j: (l, 0, 0, j)),
        ],
        out_specs=pl.BlockSpec((tm, tn), lambda i, j: (i, j)),
        scratch_shapes=[pltpu.VMEM((3, tm, MIX_W), bf16)],
        compiler_params=_cp("parallel", "arbitrary"),
        name="merge",
    )(ys[0], ys[1], ys[2], hid, hid, hid, wb)


def _attn_kernel(q_ref, k_ref, v_ref, *rest, nseq, rows, interleaved):
    o_ref = rest[-1]
    scale = X_HEAD_DIM ** -0.5
    nct = X_HEAD_DIM // LANES

    def seq_body(g, carry):
        rs = pl.ds(pl.multiple_of(g * rows, 8), rows)
        for h in range(X_HEADS):
            cs = slice(h * X_HEAD_DIM, (h + 1) * X_HEAD_DIM)
            q = q_ref[rs, cs].astype(bf16)
            if interleaved:
                def tile(ref, c):
                    return ref[g, pl.ds(c * X_HEADS + h, MEM_LEN, stride=X_HEADS * nct), :].astype(bf16)
                s = _dot_nt(q[:, 0:LANES], tile(k_ref, 0))
                for c in range(1, nct):
                    s = s + _dot_nt(q[:, c * LANES:(c + 1) * LANES], tile(k_ref, c))
            else:
                s = _dot_nt(q, k_ref[g, :, cs].astype(bf16))
            s = s * scale
            s = s - jnp.max(s, axis=-1, keepdims=True)
            e = jnp.exp(s)
            p = (e / jnp.sum(e, axis=-1, keepdims=True)).astype(bf16)
            if interleaved:
                for c in range(nct):
                    o_ref[rs, h * X_HEAD_DIM + c * LANES:h * X_HEAD_DIM + (c + 1) * LANES] = _dot(p, tile(v_ref, c))
            else:
                o_ref[rs, cs] = _dot(p, v_ref[g, :, cs].astype(bf16))
        return carry

    lax.fori_loop(0, nseq, seq_body, 0)


def _attention(q, mk, mv, l, row0, nb, seq_len, nseq, rows, o_prev=None):
    r = nseq * rows
    steps_per_seq = seq_len // rows if nseq == 1 else 1
    rb0 = row0 // r
    interleaved = mk.ndim == 4
    if interleaved:
        kv_spec = pl.BlockSpec((None, nseq, mk.shape[2], LANES), lambda b, t: (l, b, 0, 0))
    else:
        kv_spec = pl.BlockSpec((nseq, MEM_LEN, D_MODEL), lambda b, t: (b, 0, 0))
    return pl.pallas_call(
        functools.partial(_attn_kernel, nseq=nseq, rows=rows, interleaved=interleaved),
        out_shape=jax.ShapeDtypeStruct(q.shape, f32),
        grid=(nb // nseq, steps_per_seq),
        in_specs=[
            pl.BlockSpec((r, D_MODEL), lambda b, t: (rb0 + b * steps_per_seq + t, 0)),
            kv_spec, kv_spec,
        ] + ([pl.BlockSpec(memory_space=pl.ANY)] if o_prev is not None else []),
        out_specs=pl.BlockSpec((r, D_MODEL), lambda b, t: (rb0 + b * steps_per_seq + t, 0)),
        input_output_aliases=({3: 0} if o_prev is not None else {}),
        compiler_params=_cp("parallel", "arbitrary"),
        name="mem_attn",
    )(*([q, mk, mv] + ([o_prev] if o_prev is not None else [])))


ROW_BLOCK = 128


def _cumsum_rows(tri_b, x):
    hi = x.astype(bf16)
    r1 = x - hi.astype(f32)
    mid = r1.astype(bf16)
    lo = (r1 - mid.astype(f32)).astype(bf16)
    return _dot(tri_b, hi) + _dot(tri_b, mid) + _dot(tri_b, lo)


def _chunk_tril(rb, cs):
    r = lax.broadcasted_iota(jnp.int32, (rb, rb), 0)
    c = lax.broadcasted_iota(jnp.int32, (rb, rb), 1)
    return (r >= c) & ((r ^ c) < cs)


def _rows_of(x, first, width):
    parts = [jnp.broadcast_to(x[j * width + first:j * width + first + 1, :], (width, x.shape[1]))
             for j in range(x.shape[0] // width)]
    return parts[0] if len(parts) == 1 else jnp.concatenate(parts, axis=0)


def _ssd_kernel(*refs, cs, nblk, per_seq_state, n_alias):
    (z_ref, xs_ref, bm_ref, cm_ref, dt_ref, cx0_ref, cb0_ref, cc0_ref, h0_ref,
     wx_ref, wb_ref, wc_ref, bx_ref, bb_ref, bc_ref, hp_ref, nrm_ref) = refs[:17]
    (y_ref, cxo_ref, cbo_ref, cco_ref, ho_ref,
     hx_ref, hb_ref, hc_ref, yg_ref, yi_ref, xw_ref, bs_ref, cs_ref, el_ref) = refs[17 + n_alias:]
    rb = ROW_BLOCK
    spb = rb // cs

    @pl.when(pl.program_id(2) == 0)
    def _():
        ho_ref[...] = h0_ref[...]
        cxo_ref[...] = cx0_ref[...]
        cbo_ref[...] = cb0_ref[...]
        cco_ref[...] = cc0_ref[...]

    hx_ref[...] = jnp.zeros_like(hx_ref)
    hb_ref[...] = jnp.zeros_like(hb_ref)
    hc_ref[...] = jnp.zeros_like(hc_ref)

    tri = _chunk_tril(rb, cs)
    tri_b = tri.astype(bf16)
    lane = lax.broadcasted_iota(jnp.int32, (rb, LANES), 1)
    left = lane < SSD_HEAD_DIM
    top = lax.broadcasted_iota(jnp.int32, (LANES, LANES), 0) < SSD_HEAD_DIM
    hp = hp_ref[...]
    a_neg = -jnp.exp(hp[1:2, :])
    npair = SSD_HEADS_PER_GROUP // 2
    nhist = SSD_CONV - 1

    def block_body(blk, carry):
        r0 = pl.multiple_of(blk * rb, rb)
        rs = pl.ds(r0, rb)

        def slot_of(j):
            return blk * spb + j if per_seq_state else 0

        def conv(h_ref, st_ref, u_ref, w_ref, b_ref):
            u = u_ref[rs, :]
            for j in range(spb):
                h_ref[j * cs:j * cs + nhist, :] = st_ref[slot_of(j)]
            hist = h_ref[...]
            pos = lax.broadcasted_iota(jnp.int32, u.shape, 0) & (cs - 1)
            out = b_ref[...] + u * w_ref[nhist:nhist + 1, :]
            for r in range(1, SSD_CONV):
                up = (rb - (nhist - r)) % rb
                older = jnp.where(pos >= r, pltpu.roll(u, r, axis=0), pltpu.roll(hist, up, axis=0) if up else hist)
                out = out + older * w_ref[nhist - r:nhist - r + 1, :]
            for j in range(spb):
                st_ref[slot_of(j)] = u[j * cs + cs - nhist:j * cs + cs, :]
            return _silu(out)

        xc = conv(hx_ref, cxo_ref, xs_ref, wx_ref, bx_ref)
        bmat = conv(hb_ref, cbo_ref, bm_ref, wb_ref, bb_ref)
        cmat = conv(hc_ref, cco_ref, cm_ref, wc_ref, bc_ref)
        bs_ref[...] = bmat
        cs_ref[...] = cmat
        bmb = bmat.astype(bf16)
        cmb = cmat.astype(bf16)

        dt = _softplus(dt_ref[rs, :] + hp[0:1, :])
        acum = _cumsum_rows(tri_b, dt * a_neg)
        acum_t = acum.T
        dt_t = dt.T
        a_last = _rows_of(acum, cs - 1, cs)
        ea = jnp.exp(acum)
        tail = jnp.exp(a_last - acum) * dt
        el_ref[...] = jnp.exp(a_last)
        cb = _dot_nt(cmb, bmb)

        def pair_lanes(x, pp):
            return jnp.where(left, x[:, 2 * pp:2 * pp + 1], x[:, 2 * pp + 1:2 * pp + 2])

        for pp in range(npair):
            ls = slice(pp * LANES, (pp + 1) * LANES)
            xp = xc[:, ls]
            xpb = xp.astype(bf16)

            def head_w(j):
                seg = acum[:, j:j + 1] - acum_t[j:j + 1, :]
                dec = jnp.exp(jnp.where(tri, seg, -jnp.inf))
                return (cb * dec * dt_t[j:j + 1, :]).astype(bf16)

            zero = jnp.zeros_like(xpb)
            y = _dot(head_w(2 * pp), jnp.where(left, xpb, zero)) + _dot(head_w(2 * pp + 1), jnp.where(left, zero, xpb))
            yg_ref[:, ls] = y + xp * pair_lanes(hp[2:3, :], pp)
            xw_ref[:, ls] = xp * pair_lanes(tail, pp)

        def seq_body(j, c):
            slot = slot_of(j)
            rl = pl.ds(pl.multiple_of(j * cs, 8), cs)
            cj = cs_ref[rl, :].astype(bf16)
            bj = bs_ref[rl, :].astype(bf16)
            el = el_ref[pl.ds(j * cs, 1), :]
            for pp in range(npair):
                ls = slice(pp * LANES, (pp + 1) * LANES)
                hstate = ho_ref[slot, ls, :]
                yi_ref[rl, ls] = _dot_nt(cj, hstate.astype(bf16))
                dec_rows = jnp.where(top, el[:, 2 * pp:2 * pp + 1], el[:, 2 * pp + 1:2 * pp + 2])
                ho_ref[slot, ls, :] = hstate * dec_rows + _dot_tn(xw_ref[rl, ls].astype(bf16), bj)
            return c

        lax.fori_loop(0, spb, seq_body, 0, unroll=min(spb, 2))

        for pp in range(npair):
            ls = slice(pp * LANES, (pp + 1) * LANES)
            yg_ref[:, ls] = (yg_ref[:, ls] + yi_ref[:, ls] * pair_lanes(ea, pp)) * _silu(z_ref[rs, ls])
        y_ref[rs, :] = _rms_norm(yg_ref[...], nrm_ref[...]).astype(y_ref.dtype)
        return carry

    lax.fori_loop(0, nblk, block_body, 0)


def _ssd(hid, conv_st, h_st, l, st_l, conv_w, conv_b, hp, nrm, prev, y_prev, *, row0, nb, seq_len, cs, nblk):
    rows = nblk * ROW_BLOCK
    per_seq_state = seq_len == cs
    g = rows // seq_len if per_seq_state else 1
    nt = 1 if per_seq_state else seq_len // rows
    rb0 = row0 // rows
    gw = SSD_GROUP_W
    extra = [a for a in (prev, y_prev) if a is not None]

    def rowblk(b, t):
        return rb0 + b * nt + t

    def hid_spec(col0, width):
        return pl.BlockSpec((rows, width), lambda b, gi, t: (rowblk(b, t), col0 // width + gi))

    def st_spec(col0, width):
        return pl.BlockSpec((None, g, 3, width), lambda b, gi, t: (st_l, b, 0, col0 // width + gi))

    def w_spec(nrow, col0, width):
        return pl.BlockSpec((None, nrow, width), lambda b, gi, t: (l, 0, col0 // width + gi))

    def sto_spec(width):
        return pl.BlockSpec((g, 3, width), lambda b, gi, t: (b, 0, gi))

    kern = functools.partial(_ssd_kernel, cs=cs, nblk=nblk, per_seq_state=per_seq_state, n_alias=len(extra))
    args = [hid, hid, hid, hid, hid, conv_st, conv_st, conv_st, h_st,
            conv_w, conv_w, conv_w, conv_b, conv_b, conv_b, hp, nrm] + extra
    aliases = {}
    if prev is not None:
        aliases[17] = 4
    if y_prev is not None:
        aliases[len(args) - 1] = 0
    nbc = SSD_GROUPS * SSD_STATE
    rbk = ROW_BLOCK
    return pl.pallas_call(
        kern,
        out_shape=(jax.ShapeDtypeStruct((hid.shape[0], MIX_W), f32),
                   jax.ShapeDtypeStruct((nb, 3, MIX_W), f32),
                   jax.ShapeDtypeStruct((nb, 3, nbc), f32),
                   jax.ShapeDtypeStruct((nb, 3, nbc), f32),
                   jax.ShapeDtypeStruct((DEPTH, nb, SSD_GROUPS, gw, SSD_STATE), f32)),
        grid=(nb // g, SSD_GROUPS, nt),
        input_output_aliases=aliases,
        in_specs=[
            hid_spec(COL_Z, gw), hid_spec(COL_XS, gw), hid_spec(COL_BM, LANES), hid_spec(COL_CM, LANES),
            hid_spec(COL_DT, LANES),
            st_spec(0, gw), st_spec(MIX_W, LANES), st_spec(MIX_W + nbc, LANES),
            pl.BlockSpec((None, g, None, gw, SSD_STATE), lambda b, gi, t: (st_l, b, gi, 0, 0)),
            w_spec(SSD_CONV, 0, gw), w_spec(SSD_CONV, MIX_W, LANES), w_spec(SSD_CONV, MIX_W + nbc, LANES),
            w_spec(1, 0, gw), w_spec(1, MIX_W, LANES), w_spec(1, MIX_W + nbc, LANES),
            pl.BlockSpec((None, None, 8, LANES), lambda b, gi, t: (l, gi, 0, 0)),
            pl.BlockSpec((None, 1, gw), lambda b, gi, t: (l, 0, gi)),
        ] + [pl.BlockSpec(memory_space=pl.ANY)] * len(extra),
        out_specs=(
            pl.BlockSpec((rows, gw), lambda b, gi, t: (rowblk(b, t), gi)),
            sto_spec(gw), sto_spec(LANES), sto_spec(LANES),
            pl.BlockSpec((None, g, None, gw, SSD_STATE), lambda b, gi, t: (l, b, gi, 0, 0)),
        ),
        scratch_shapes=[pltpu.VMEM((rbk, gw), f32), pltpu.VMEM((rbk, LANES), f32), pltpu.VMEM((rbk, LANES), f32),
                        pltpu.VMEM((rbk, gw), f32), pltpu.VMEM((rbk, gw), f32), pltpu.VMEM((rbk, gw), f32),
                        pltpu.VMEM((rbk, LANES), f32), pltpu.VMEM((rbk, LANES), f32), pltpu.VMEM((rbk, LANES), f32)],
        compiler_params=_cp("parallel", "parallel", "arbitrary"),
        name="ssd_scan",
    )(*args)


def _mlstm_kernel(*refs, cs, nblk, per_seq_state, n_alias):
    q_ref, k_ref, v_ref, o_ref, gt_ref, c0_ref, n0_ref, m0_ref, gb_ref, nrm_ref = refs[:10]
    y_ref, co_ref, no_ref, mo_ref, num_ref, kw_ref, dec_ref, mt_ref, sp_ref = refs[10 + n_alias:]
    rb = ROW_BLOCK
    spb = rb // cs

    @pl.when(pl.program_id(2) == 0)
    def _():
        co_ref[...] = c0_ref[...]
        no_ref[...] = n0_ref[...]
        mo_ref[...] = m0_ref[...]

    tri = _chunk_tril(rb, cs)
    tri_b = tri.astype(bf16)
    kscale = ML_DK ** -0.5

    def block_body(blk, carry):
        r0 = pl.multiple_of(blk * rb, rb)
        rs = pl.ds(r0, rb)

        def slot_of(j):
            return blk * spb + j if per_seq_state else 0

        def per_seq(ref, width):
            parts = [jnp.broadcast_to(ref[slot_of(j)], (cs, width)) for j in range(spb)]
            return parts[0] if spb == 1 else jnp.concatenate(parts, axis=0)

        gates = gt_ref[rs, :] + gb_ref[...]
        bfull = _cumsum_rows(tri_b, _log_sigmoid(gates))
        logi_c = gates[:, 0:1]
        b_c = bfull[:, 1:2]
        logi_r = gates.T[0:1, :]
        b_r = bfull.T[1:2, :]
        m_prev = per_seq(mo_ref, 1)
        n_prev = per_seq(no_ref, ML_DK)

        dmat = jnp.where(tri, b_c - b_r + logi_r, -jnp.inf)
        prev = b_c + m_prev
        mt = jnp.maximum(prev, jnp.max(dmat, axis=-1, keepdims=True))
        wts = jnp.exp(dmat - mt)
        sprev = jnp.exp(prev - mt)

        qh = q_ref[rs, :]
        kh = k_ref[rs, :] * kscale
        qk = _dot_nt(qh.astype(bf16), kh.astype(bf16)) * wts
        num_ref[...] = _dot(qk.astype(bf16), v_ref[rs, :].astype(bf16))
        den = jnp.sum(qk, axis=-1, keepdims=True) + jnp.sum(qh * n_prev, axis=-1, keepdims=True) * sprev

        m_new = _rows_of(mt, cs - 1, cs)
        b_last = _rows_of(b_c, cs - 1, cs)
        kw_ref[...] = kh * jnp.exp(b_last - b_c + logi_c - m_new)
        dec_ref[...] = jnp.broadcast_to(jnp.exp(b_last + m_prev - m_new), (rb, LANES))
        mt_ref[...] = jnp.broadcast_to(m_new, (rb, LANES))
        sp_ref[...] = jnp.broadcast_to(sprev, (rb, LANES))

        def seq_body(j, c):
            slot = slot_of(j)
            rl = pl.ds(pl.multiple_of(j * cs, 8), cs)
            rg = pl.ds(pl.multiple_of(r0 + j * cs, 8), cs)
            cst = co_ref[slot]
            num_ref[rl, :] += _dot(q_ref[rg, :].astype(bf16), cst.astype(bf16)) * sp_ref[rl, 0:1]
            kw = kw_ref[rl, :]
            dec = dec_ref[pl.ds(j * cs, 1), 0:1]
            co_ref[slot] = cst * dec + _dot_tn(kw.astype(bf16), v_ref[rg, :].astype(bf16))
            no_ref[slot] = no_ref[slot] * dec + jnp.sum(kw, axis=0, keepdims=True)
            mo_ref[slot] = mt_ref[pl.ds(j * cs, 1), 0:1]
            return c

        lax.fori_loop(0, spb, seq_body, 0, unroll=min(spb, 2))

        hh = num_ref[...] / jnp.maximum(jnp.abs(den), jnp.exp(-mt))
        y = _rms_norm(hh, nrm_ref[...]) * _sigmoid(o_ref[rs, :])
        y_ref[rs, :] = y.astype(y_ref.dtype)
        return carry

    lax.fori_loop(0, nblk, block_body, 0)


def _mlstm(hid, c_st, n_st, m_st, l, st_l, gbias, nrm, prev, y_prev, *, row0, nb, seq_len, cs, nblk):
    rows = nblk * ROW_BLOCK
    per_seq_state = seq_len == cs
    g = rows // seq_len if per_seq_state else 1
    nt = 1 if per_seq_state else seq_len // rows
    rb0 = row0 // rows
    extra = [a for a in (prev, y_prev) if a is not None]

    def hid_spec(col0, width):
        return pl.BlockSpec((rows, width), lambda b, h, t: (rb0 + b * nt + t, col0 // width + h))

    kern = functools.partial(_mlstm_kernel, cs=cs, nblk=nblk, per_seq_state=per_seq_state, n_alias=len(extra))
    args = [hid, hid, hid, hid, hid, c_st, n_st, m_st, gbias, nrm] + extra
    aliases = {}
    if prev is not None:
        aliases[10] = 1
    if y_prev is not None:
        aliases[len(args) - 1] = 0
    rbk = ROW_BLOCK
    return pl.pallas_call(
        kern,
        input_output_aliases=aliases,
        out_shape=(jax.ShapeDtypeStruct((hid.shape[0], MIX_W), f32),
                   jax.ShapeDtypeStruct((DEPTH, nb, ML_HEADS, ML_DK, ML_DV), f32),
                   jax.ShapeDtypeStruct((nb, ML_HEADS, 1, ML_DK), f32),
                   jax.ShapeDtypeStruct((nb, ML_HEADS, 1, 1), f32)),
        grid=(nb // g, ML_HEADS, nt),
        in_specs=[
            hid_spec(COL_MQ, ML_DK), hid_spec(COL_MK, ML_DK), hid_spec(COL_MV, ML_DV), hid_spec(COL_MO, ML_DV),
            hid_spec(COL_MG, LANES),
            pl.BlockSpec((None, g, None, ML_DK, ML_DV), lambda b, h, t: (st_l, b, h, 0, 0)),
            pl.BlockSpec((None, g, None, 1, ML_DK), lambda b, h, t: (st_l, b, h, 0, 0)),
            pl.BlockSpec((None, g, None, 1, 1), lambda b, h, t: (st_l, b, h, 0, 0)),
            pl.BlockSpec((None, None, 1, LANES), lambda b, h, t: (l, h, 0, 0)),
            pl.BlockSpec((None, 1, ML_DV), lambda b, h, t: (l, 0, h)),
        ] + [pl.BlockSpec(memory_space=pl.ANY)] * len(extra),
        out_specs=(
            pl.BlockSpec((rows, ML_DV), lambda b, h, t: (rb0 + b * nt + t, h)),
            pl.BlockSpec((None, g, None, ML_DK, ML_DV), lambda b, h, t: (l, b, h, 0, 0)),
            pl.BlockSpec((g, None, 1, ML_DK), lambda b, h, t: (b, h, 0, 0)),
            pl.BlockSpec((g, None, 1, 1), lambda b, h, t: (b, h, 0, 0)),
        ),
        scratch_shapes=[pltpu.VMEM((rbk, ML_DV), f32), pltpu.VMEM((rbk, ML_DK), f32), pltpu.VMEM((rbk, LANES), f32),
                        pltpu.VMEM((rbk, LANES), f32), pltpu.VMEM((rbk, LANES), f32)],
        compiler_params=_cp("parallel", "parallel", "arbitrary"),
        name="mlstm_scan",
    )(*args)


HG_ROWS = ROW_BLOCK
HG_TILE = 8


def _hgrn_kernel(*refs, cs, nblk, per_seq_state, layer, n_alias):
    q_ref, f_ref, i_ref, g_ref, s0_ref, lg_ref, nrm_ref = refs[:7]
    y_ref, so_ref, b_ref, qe_ref, kd_ref, v_ref, yi_ref = refs[7 + n_alias:]
    rb = HG_ROWS
    spb = rb // cs

    @pl.when(pl.program_id(2) == 0)
    def _():
        so_ref[...] = s0_ref[...]

    lg = lg_ref[...]
    pe = jnp.exp(lg - jnp.max(lg, axis=0, keepdims=True))
    pw = pe / jnp.sum(pe, axis=0, keepdims=True)
    lb = jnp.zeros((1, HG_DK), f32)
    for j in range(1, layer + 1):
        lb = lb + pw[j:j + 1, :]
    log_lb = jnp.log(lb)
    log_1m = jnp.log1p(-lb)

    row_a = lax.broadcasted_iota(jnp.int32, (rb, rb), 0)
    lane_a = lax.broadcasted_iota(jnp.int32, (rb, rb), 1)
    row_c = lax.broadcasted_iota(jnp.int32, (rb, HG_DK), 0)
    tri_b = ((row_a >= lane_a) & ((row_a ^ lane_a) < cs)).astype(bf16)

    def block_body(blk, carry):
        rs = pl.ds(pl.multiple_of(blk * rb, rb), rb)
        fr = f_ref[rs, :]
        u = log_1m + _log_sigmoid(fr)
        mx = jnp.maximum(log_lb, u)
        mn = jnp.minimum(log_lb, u)
        logf = mx + jnp.log1p(jnp.exp(mn - mx))
        kk = (1.0 - lb) * _sigmoid(-fr)
        qq = _silu(q_ref[rs, :])
        v = i_ref[rs, :]
        b = _cumsum_rows(tri_b, logf)

        att = jnp.where(lane_a == row_a, jnp.sum(qq * kk, axis=-1, keepdims=True), 0.0)
        for r in range(1, HG_TILE):
            valid = (row_c & (HG_TILE - 1)) >= r
            e = jnp.exp(jnp.where(valid, b - pltpu.roll(b, r, axis=0), 0.0))
            col = jnp.sum(jnp.where(valid, qq * e * pltpu.roll(kk, r, axis=0), 0.0), axis=-1, keepdims=True)
            att = att + jnp.where(lane_a == row_a - r, col, 0.0)
        size = cs
        while size > HG_TILE:
            half = size // 2
            e = jnp.exp(-jnp.abs(b - _rows_of(b, half - 1, size)))
            second = (row_c & half) != 0
            q_t = jnp.where(second, qq * e, 0.0).astype(bf16)
            k_t = jnp.where(second, 0.0, kk * e).astype(bf16)
            att = att + jnp.where((row_a ^ lane_a) < size, _dot_nt(q_t, k_t), 0.0)
            size = half

        yi_ref[...] = _dot(att.astype(bf16), v.astype(bf16))
        b_ref[...] = b
        qe_ref[...] = qq * jnp.exp(b)
        kd_ref[...] = kk * jnp.exp(_rows_of(b, cs - 1, cs) - b)
        v_ref[...] = v

        def seq_body(j, c):
            slot = blk * spb + j if per_seq_state else 0
            r = pl.ds(pl.multiple_of(j * cs, 8), cs)
            st = so_ref[slot]
            yi_ref[r, :] += _dot(qe_ref[r, :].astype(bf16), st.astype(bf16))
            b_last = b_ref[pl.ds(j * cs + cs - 1, 1), :]
            dcol = jnp.exp(jnp.broadcast_to(b_last, (8, HG_DK))).T[:, 0:1]
            so_ref[slot] = st * dcol + _dot_tn(kd_ref[r, :].astype(bf16), v_ref[r, :].astype(bf16))
            return c

        lax.fori_loop(0, spb, seq_body, 0, unroll=min(spb, 2))
        y = _rms_norm(yi_ref[...], nrm_ref[...]) * _sigmoid(g_ref[rs, :])
        y_ref[rs, :] = y.astype(y_ref.dtype)
        return carry

    lax.fori_loop(0, nblk, block_body, 0)


def _hgrn(hid, s_st, l, st_l, logits, nrm, prev, y_prev, *, row0, nb, seq_len, cs, nblk):
    rows = nblk * HG_ROWS
    per_seq_state = seq_len == cs
    g = rows // seq_len if per_seq_state else 1
    nt = 1 if per_seq_state else seq_len // rows
    rb0 = row0 // rows
    extra = [a for a in (prev, y_prev) if a is not None]

    def hid_spec(col0):
        return pl.BlockSpec((rows, LANES), lambda b, h, t: (rb0 + b * nt + t, col0 // LANES + h))

    kern = functools.partial(_hgrn_kernel, cs=cs, nblk=nblk, per_seq_state=per_seq_state, layer=l, n_alias=len(extra))
    in_specs = [
        hid_spec(COL_HQ), hid_spec(COL_HF), hid_spec(COL_HI), hid_spec(COL_HG),
        pl.BlockSpec((None, g, None, HG_DK, HG_DV), lambda b, h, t: (st_l, b, h, 0, 0)),
        pl.BlockSpec((DEPTH, LANES), lambda b, h, t: (0, h)),
        pl.BlockSpec((None, 1, LANES), lambda b, h, t: (l, 0, h)),
    ]
    args = [hid, hid, hid, hid, s_st, logits, nrm] + extra
    in_specs += [pl.BlockSpec(memory_space=pl.ANY)] * len(extra)
    aliases = {}
    if prev is not None:
        aliases[7] = 1
    if y_prev is not None:
        aliases[len(args) - 1] = 0
    return pl.pallas_call(
        kern,
        out_shape=(jax.ShapeDtypeStruct((hid.shape[0], MIX_W), f32),
                   jax.ShapeDtypeStruct((DEPTH, nb, HG_HEADS, HG_DK, HG_DV), f32)),
        grid=(nb // g, HG_HEADS, nt),
        in_specs=in_specs,
        out_specs=(
            pl.BlockSpec((rows, LANES), lambda b, h, t: (rb0 + b * nt + t, h)),
            pl.BlockSpec((None, g, None, HG_DK, HG_DV), lambda b, h, t: (l, b, h, 0, 0)),
        ),
        scratch_shapes=[pltpu.VMEM((HG_ROWS, HG_DK), f32)] * 5,
        input_output_aliases=aliases,
        compiler_params=_cp("parallel", "parallel", "arbitrary"),
        name="hgrn_scan",
    )(*args)


def _prep_w_in(w_in):
    off = {}
    start = 0
    names = ("z", "xbc", "dt", "mq", "mk", "mv", "mo", "mi", "mf", "hq", "hf", "hi", "hg", "gate")
    sizes = (MIX_W, MIX_W + 2 * SSD_GROUPS * SSD_STATE, SSD_HEADS, ML_HEADS * ML_DK, ML_HEADS * ML_DK, MIX_W, MIX_W,
             ML_HEADS, ML_HEADS, MIX_W, MIX_W, MIX_W, MIX_W, 3 * D_MODEL)
    for nme, sz in zip(names, sizes):
        off[nme] = (start, sz)
        start += sz

    def seg(nme, lo=0, hi=None):
        s, sz = off[nme]
        hi = sz if hi is None else hi
        return w_in[:, :, s + lo:s + hi]

    nbc = SSD_GROUPS * SSD_STATE
    dpth, kdim = w_in.shape[0], w_in.shape[1]
    dt = seg("dt").reshape(dpth, kdim, SSD_GROUPS, SSD_HEADS_PER_GROUP)
    dt = jnp.pad(dt, ((0, 0), (0, 0), (0, 0), (0, LANES - SSD_HEADS_PER_GROUP))).reshape(dpth, kdim, SSD_GROUPS * LANES)
    mg = jnp.stack([seg("mi"), seg("mf")], axis=-1)
    mg = jnp.pad(mg, ((0, 0), (0, 0), (0, 0), (0, LANES - 2))).reshape(dpth, kdim, ML_HEADS * LANES)
    cols = [seg("z"), seg("xbc", 0, MIX_W), seg("mv"), seg("mo"), seg("hq"), seg("hf"), seg("hi"), seg("hg"),
            seg("gate"), seg("mq"), seg("mk"), seg("xbc", MIX_W, MIX_W + nbc), seg("xbc", MIX_W + nbc, MIX_W + 2 * nbc),
            dt, mg]
    return jnp.concatenate(cols, axis=-1).astype(bf16)


def _row_major_groups(x_prompt, x_sample):
    return jnp.concatenate([x_prompt.reshape(-1, D_MODEL), x_sample.reshape(-1, D_MODEL)], axis=0)


def kernel(x_prompt, x_sample, mem_prompt, state_ssd_conv, state_ssd, state_mlstm_c, state_mlstm_n, state_mlstm_m, state_hgrn, cache_mem_k, cache_mem_v, ln_g, ln_b, ffn_w1, ffn_w3, ffn_w2, w_in, ssd_conv_w, ssd_conv_b, ssd_dt_bias, ssd_a_log, ssd_d, ssd_norm, ml_gate_bias, ml_norm, hg_lb_logits, hg_norm, w_branch, w_mix_out, x_wq, x_wk, x_wv, x_wo):
    nbp, lp = x_prompt.shape[0], x_prompt.shape[1]
    nbs, ls = x_sample.shape[0], x_sample.shape[1]
    mp = nbp * lp
    depth = w_in.shape[0]

    w1b = ffn_w1.astype(bf16)
    w3b = ffn_w3.astype(bf16)
    w2b = ffn_w2.astype(bf16)
    w_hid = _prep_w_in(w_in)
    wb = w_branch.astype(bf16)
    w_mo = w_mix_out.astype(bf16)
    w_q = x_wq.astype(bf16)
    w_o = x_wo.astype(bf16)
    w_kv = jnp.concatenate([x_wk, x_wv], axis=-1).astype(bf16)
    lng = ln_g.reshape(depth, 4, 1, D_MODEL)
    lnb = ln_b.reshape(depth, 4, 1, D_MODEL)

    def per_head_rows(*rows):
        r = jnp.stack([a.reshape(depth, SSD_GROUPS, SSD_HEADS_PER_GROUP) for a in rows], axis=2)
        return jnp.pad(r, ((0, 0), (0, 0), (0, 8 - len(rows)), (0, LANES - SSD_HEADS_PER_GROUP)))

    ssd_hp = per_head_rows(ssd_dt_bias, ssd_a_log, ssd_d)
    ssd_nrm = ssd_norm.reshape(depth, 1, MIX_W)
    conv_b = ssd_conv_b.reshape(depth, 1, -1)
    ml_gb = jnp.pad(jnp.swapaxes(ml_gate_bias, 1, 2), ((0, 0), (0, 0), (0, LANES - 2))).reshape(depth, ML_HEADS, 1, LANES)
    ml_nrm = ml_norm.reshape(depth, 1, MIX_W)
    hg_nrm = hg_norm.reshape(depth, 1, MIX_W)

    s_ssd = state_ssd.reshape(depth, nbs, SSD_GROUPS, SSD_GROUP_W, SSD_STATE)
    s_n = state_mlstm_n.reshape(depth, nbs, ML_HEADS, 1, ML_DK)
    s_m = state_mlstm_m.reshape(depth, nbs, ML_HEADS, 1, 1)
    nct = X_HEAD_DIM // LANES

    def cache_rows(c):
        c = c.reshape(depth, nbs, MEM_LEN, X_HEADS, nct, LANES)
        return jnp.swapaxes(c, 3, 4).reshape(depth, nbs, MEM_LEN * nct * X_HEADS, LANES)

    cache_k = cache_rows(cache_mem_k)
    cache_v = cache_rows(cache_mem_v)
    z_conv = jnp.zeros((1, nbp, SSD_CONV - 1, state_ssd_conv.shape[-1]), f32)
    z_ssd = jnp.zeros((1, nbp, SSD_GROUPS, SSD_GROUP_W, SSD_STATE), f32)
    z_c = jnp.zeros((1, nbp, ML_HEADS, ML_DK, ML_DV), f32)
    z_n = jnp.zeros((1, nbp, ML_HEADS, 1, ML_DK), f32)
    z_m = jnp.zeros((1, nbp, ML_HEADS, 1, 1), f32)
    z_hg = jnp.zeros((1, nbp, HG_HEADS, HG_DK, HG_DV), f32)

    mem_rows = mem_prompt.reshape(nbp * MEM_LEN, D_MODEL)
    x = _row_major_groups(x_prompt, x_sample)

    p_small = [[] for _ in range(5)]
    s_small = [[] for _ in range(3)]
    p_ssd = p_c = p_hg = s_ssd_o = s_c_o = s_hg_o = None
    pc, sc = 128, ls
    blocks_per_seq = lp // ROW_BLOCK
    pb = min(8, blocks_per_seq)
    pb_hg = min(16, blocks_per_seq)
    sb_hg = min(4, nbs * ls // ROW_BLOCK)

    for l in range(depth):
        x = _ffn_ln(x, w1b, w3b, w2b, lng, lnb, l, 0, 0)
        hid = _matmul(x, w_hid, l, f32, tm=1024, tn=1024, name="in_proj")

        yp, cx, cb_, cc, p_ssd = _ssd(hid, z_conv, z_ssd, l, 0, ssd_conv_w, conv_b, ssd_hp, ssd_nrm, p_ssd, None,
                                      row0=0, nb=nbp, seq_len=lp, cs=pc, nblk=pb)
        y_ssd, sx, sb, scc, s_ssd_o = _ssd(hid, state_ssd_conv, s_ssd, l, l, ssd_conv_w, conv_b, ssd_hp, ssd_nrm, s_ssd_o,
                                           yp, row0=mp, nb=nbs, seq_len=ls, cs=sc, nblk=1)
        p_small[0].append(jnp.concatenate([cx, cb_, cc], axis=-1))
        s_small[0].append(jnp.concatenate([sx, sb, scc], axis=-1))

        yp, p_c, n1, m1 = _mlstm(hid, z_c, z_n, z_m, l, 0, ml_gb, ml_nrm, p_c, None,
                                 row0=0, nb=nbp, seq_len=lp, cs=pc, nblk=pb)
        y_ml, s_c_o, sn1, sm1 = _mlstm(hid, state_mlstm_c, s_n, s_m, l, l, ml_gb, ml_nrm, s_c_o, yp,
                                       row0=mp, nb=nbs, seq_len=ls, cs=sc, nblk=1)
        p_small[1].append(n1.reshape(nbp, ML_HEADS, ML_DK))
        p_small[2].append(m1.reshape(nbp, ML_HEADS))
        s_small[1].append(sn1.reshape(nbs, ML_HEADS, ML_DK))
        s_small[2].append(sm1.reshape(nbs, ML_HEADS))

        yp, p_hg = _hgrn(hid, z_hg, l, 0, hg_lb_logits, hg_nrm, p_hg, None,
                         row0=0, nb=nbp, seq_len=lp, cs=pc, nblk=pb_hg)
        y_hg, s_hg_o = _hgrn(hid, state_hgrn, l, l, hg_lb_logits, hg_nrm, s_hg_o, yp,
                             row0=mp, nb=nbs, seq_len=ls, cs=sc, nblk=sb_hg)

        s_mix = _merge((y_ssd, y_ml, y_hg), hid, wb, l)
        x = _res_ln(x, s_mix, w_mo, lng, lnb, l, 1)

        kv = _matmul(mem_rows, w_kv, l, f32, tm=nbp * MEM_LEN, name="mem_kv")
        pk = kv[:, :D_MODEL].reshape(nbp, MEM_LEN, D_MODEL)
        pv = kv[:, D_MODEL:].reshape(nbp, MEM_LEN, D_MODEL)
        p_small[3].append(pk.reshape(nbp, MEM_LEN, X_HEADS, X_HEAD_DIM))
        p_small[4].append(pv.reshape(nbp, MEM_LEN, X_HEADS, X_HEAD_DIM))

        q = _matmul(x, w_q, l, f32, tm=1024, name="q_proj")
        op = _attention(q, pk, pv, l, 0, nbp, lp, nseq=1, rows=512)
        o = _attention(q, cache_k, cache_v, l, mp, nbs, ls, nseq=4, rows=ls, o_prev=op)
        x = _res_ln(x, o, w_o, lng, lnb, l, 2)
        x = _ffn_ln(x, w1b, w3b, w2b, lng, lnb, l, 1, 3)

    y_prompt = x[:mp].reshape(nbp, lp, D_MODEL)
    y_sample = x[mp:].reshape(nbs, ls, D_MODEL)
    st = jnp.stack
    return (y_prompt, y_sample,
            st(p_small[0]), p_ssd.reshape(depth, nbp, SSD_HEADS, SSD_HEAD_DIM, SSD_STATE), p_c, st(p_small[1]),
            st(p_small[2]), p_hg, st(p_small[3]), st(p_small[4]),
            st(s_small[0]), s_ssd_o.reshape(depth, nbs, SSD_HEADS, SSD_HEAD_DIM, SSD_STATE), s_c_o, st(s_small[1]),
            st(s_small[2]), s_hg_o)
```

```python
import functools

import jax
import jax.numpy as jnp
from jax import lax
from jax.experimental import pallas as pl
from jax.experimental.pallas import tpu as pltpu

f32 = jnp.float32
bf16 = jnp.bfloat16

D_MODEL = 2048
DEPTH = 2
MIX_W = D_MODEL
SSD_HEAD_DIM = 64
SSD_HEADS = 32
SSD_GROUPS = 4
SSD_STATE = 128
SSD_CONV = 4
SSD_GROUP_W = MIX_W // SSD_GROUPS
SSD_HEADS_PER_GROUP = SSD_HEADS // SSD_GROUPS
ML_HEADS = 4
ML_DV = 512
ML_DK = 256
HG_HEADS = 16
HG_DK = 128
HG_DV = 128
MEM_LEN = 256
X_HEADS = 4
X_HEAD_DIM = 512
D_FF = 5504
FF_TILE = 512
FFN_ROWS = 1024
DN_ALPHA = (2.0 * DEPTH) ** 0.25
LN_EPS = 1e-5
RMS_EPS = 1e-6

LANES = 128
VMEM_LIMIT = 56 * 1024 * 1024

COL_Z = 0
COL_XS = 2048
COL_MV = 4096
COL_MO = 6144
COL_HQ = 8192
COL_HF = 10240
COL_HI = 12288
COL_HG = 14336
COL_GATE = 16384
COL_MQ = 22528
COL_MK = 23552
COL_BM = 24576
COL_CM = 25088
COL_DT = 25600
COL_MG = 26112
N_HID = 26624


def _cp(*sem):
    return pltpu.CompilerParams(dimension_semantics=sem, vmem_limit_bytes=VMEM_LIMIT)


def _dot(a, b):
    return jnp.dot(a, b, preferred_element_type=f32)


def _dot_nt(a, b):
    return lax.dot_general(a, b, (((1,), (1,)), ((), ())), preferred_element_type=f32)


def _dot_tn(a, b):
    return lax.dot_general(a, b, (((0,), (0,)), ((), ())), preferred_element_type=f32)


def _sigmoid(x):
    return 1.0 / (1.0 + jnp.exp(-x))


def _silu(x):
    return x * _sigmoid(x)


def _softplus(x):
    return jnp.maximum(x, 0.0) + jnp.log1p(jnp.exp(-jnp.abs(x)))


def _log_sigmoid(x):
    return jnp.minimum(x, 0.0) - jnp.log1p(jnp.exp(-jnp.abs(x)))


def _layer_norm(y, g, b):
    mu = jnp.mean(y, axis=-1, keepdims=True)
    yc = y - mu
    var = jnp.mean(yc * yc, axis=-1, keepdims=True)
    return yc * lax.rsqrt(var + LN_EPS) * g + b


def _rms_norm(y, g):
    return y * lax.rsqrt(jnp.mean(y * y, axis=-1, keepdims=True) + RMS_EPS) * g


def _row_tile(m, preferred):
    t = preferred
    while m % t:
        t //= 2
    return t


def _ffn_kernel(x_ref, w1_ref, w3_ref, w2_ref, g_ref, b_ref, o_ref, xb_ref, *, tf, d_ff):
    j = pl.program_id(1)
    last = pl.num_programs(1) - 1

    @pl.when(j == 0)
    def _():
        xb_ref[...] = x_ref[...].astype(bf16)
        o_ref[...] = jnp.zeros_like(o_ref)

    xb = xb_ref[...]
    h = _silu(_dot(xb, w1_ref[...])) * _dot(xb, w3_ref[...])

    if d_ff % tf == 0:
        o_ref[...] += _dot(h.astype(bf16), w2_ref[...])
    else:
        @pl.when(j < last)
        def _():
            o_ref[...] += _dot(h.astype(bf16), w2_ref[...])

        @pl.when(j == last)
        def _():
            valid = d_ff % tf
            w2 = w2_ref[...]
            hm = jnp.where(lax.broadcasted_iota(jnp.int32, (1, tf), 1) < valid, h, 0.0)
            w2m = jnp.where(lax.broadcasted_iota(jnp.int32, (tf, 1), 0) < valid, w2, jnp.zeros_like(w2))
            o_ref[...] += _dot(hm.astype(bf16), w2m)

    @pl.when(j == last)
    def _():
        y = DN_ALPHA * x_ref[...] + 0.5 * o_ref[...]
        o_ref[...] = _layer_norm(y, g_ref[...], b_ref[...])


def _ffn_ln(x, w1, w3, w2, ln_g, ln_b, l, k, ln_i, tm=FFN_ROWS, tf=FF_TILE):
    m = x.shape[0]
    tm = _row_tile(m, tm)
    d_ff = w1.shape[-1]
    return pl.pallas_call(
        functools.partial(_ffn_kernel, tf=tf, d_ff=d_ff),
        out_shape=jax.ShapeDtypeStruct((m, D_MODEL), f32),
        grid=(m // tm, pl.cdiv(d_ff, tf)),
        in_specs=[
            pl.BlockSpec((tm, D_MODEL), lambda i, j: (i, 0), pipeline_mode=pl.Buffered(1)),
            pl.BlockSpec((None, None, D_MODEL, tf), lambda i, j: (l, k, 0, j)),
            pl.BlockSpec((None, None, D_MODEL, tf), lambda i, j: (l, k, 0, j)),
            pl.BlockSpec((None, None, tf, D_MODEL), lambda i, j: (l, k, j, 0)),
            pl.BlockSpec((None, None, 1, D_MODEL), lambda i, j: (l, ln_i, 0, 0)),
            pl.BlockSpec((None, None, 1, D_MODEL), lambda i, j: (l, ln_i, 0, 0)),
        ],
        out_specs=pl.BlockSpec((tm, D_MODEL), lambda i, j: (i, 0)),
        scratch_shapes=[pltpu.VMEM((tm, D_MODEL), bf16)],
        compiler_params=_cp("parallel", "arbitrary"),
        name="ffn_ln",
    )(x, w1, w3, w2, ln_g, ln_b)


def _mm_kernel(x_ref, w_ref, o_ref, xb_ref):
    @pl.when(pl.program_id(1) == 0)
    def _():
        xb_ref[...] = x_ref[...].astype(bf16)

    o_ref[...] = _dot(xb_ref[...], w_ref[...]).astype(o_ref.dtype)


def _matmul(x, w, l, out_dtype, tm, tn=512, name="matmul"):
    m, kdim = x.shape
    n = w.shape[-1]
    tm = _row_tile(m, tm)
    return pl.pallas_call(
        _mm_kernel,
        out_shape=jax.ShapeDtypeStruct((m, n), out_dtype),
        grid=(m // tm, n // tn),
        in_specs=[
            pl.BlockSpec((tm, kdim), lambda i, j: (i, 0)),
            pl.BlockSpec((None, kdim, tn), lambda i, j: (l, 0, j)),
        ],
        out_specs=pl.BlockSpec((tm, tn), lambda i, j: (i, j)),
        scratch_shapes=[pltpu.VMEM((tm, kdim), bf16)],
        compiler_params=_cp("parallel", "arbitrary"),
        name=name,
    )(x, w)


def _res_ln_kernel(x_ref, y_ref, w_ref, g_ref, b_ref, o_ref):
    r = _dot(y_ref[...].astype(bf16), w_ref[...])
    o_ref[...] = _layer_norm(DN_ALPHA * x_ref[...] + r, g_ref[...], b_ref[...])


def _res_ln(x, y, w, ln_g, ln_b, l, ln_i, tm=512):
    m = x.shape[0]
    tm = _row_tile(m, tm)
    return pl.pallas_call(
        _res_ln_kernel,
        out_shape=jax.ShapeDtypeStruct((m, D_MODEL), f32),
        grid=(m // tm,),
        in_specs=[
            pl.BlockSpec((tm, D_MODEL), lambda i: (i, 0)),
            pl.BlockSpec((tm, D_MODEL), lambda i: (i, 0)),
            pl.BlockSpec((None, D_MODEL, D_MODEL), lambda i: (l, 0, 0)),
            pl.BlockSpec((None, None, 1, D_MODEL), lambda i: (l, ln_i, 0, 0)),
            pl.BlockSpec((None, None, 1, D_MODEL), lambda i: (l, ln_i, 0, 0)),
        ],
        out_specs=pl.BlockSpec((tm, D_MODEL), lambda i: (i, 0)),
        compiler_params=_cp("parallel"),
        name="res_ln",
    )(x, y, w, ln_g, ln_b)


def _merge_kernel(y0_ref, y1_ref, y2_ref, g0_ref, g1_ref, g2_ref, w_ref, o_ref):
    acc = _sigmoid(g0_ref[...]) * _dot(y0_ref[...].astype(bf16), w_ref[0])
    acc += _sigmoid(g1_ref[...]) * _dot(y1_ref[...].astype(bf16), w_ref[1])
    acc += _sigmoid(g2_ref[...]) * _dot(y2_ref[...].astype(bf16), w_ref[2])
    o_ref[...] = acc.astype(o_ref.dtype)


def _merge(ys, hid, wb, l, tm=512, tn=512):
    m = hid.shape[0]
    tm = _row_tile(m, tm)
    gb = COL_GATE // tn
    nb = D_MODEL // tn
    y_spec = pl.BlockSpec((tm, MIX_W), lambda i, j: (i, 0))
    return pl.pallas_call(
        _merge_kernel,
        out_shape=jax.ShapeDtypeStruct((m, D_MODEL), bf16),
        grid=(m // tm, D_MODEL // tn),
        in_specs=[
            y_spec, y_spec, y_spec,
            pl.BlockSpec((tm, tn), lambda i, j: (i, gb + j)),
            pl.BlockSpec((tm, tn), lambda i, j: (i, gb + nb + j)),
            pl.BlockSpec((tm, tn), lambda i, j: (i, gb + 2 * nb + j)),
            pl.BlockSpec((None, 3, MIX_W, tn), lambda i, j: (l, 0, 0, j)),
        ],
        out_specs=pl.BlockSpec((tm, tn), lambda i, j: (i, j)),
        compiler_params=_cp("parallel", "arbitrary"),
        name="merge",
    )(ys[0], ys[1], ys[2], hid, hid, hid, wb)


def _attn_kernel(q_ref, k_ref, v_ref, *rest, nseq, rows, interleaved):
    o_ref = rest[-1]
    scale = X_HEAD_DIM ** -0.5
    nct = X_HEAD_DIM // LANES

    def seq_body(g, carry):
        rs = pl.ds(pl.multiple_of(g * rows, 8), rows)
        for h in range(X_HEADS):
            cs = slice(h * X_HEAD_DIM, (h + 1) * X_HEAD_DIM)
            q = q_ref[rs, cs].astype(bf16)
            if interleaved:
                def tile(ref, c):
                    return ref[g, pl.ds(c * X_HEADS + h, MEM_LEN, stride=X_HEADS * nct), :].astype(bf16)
                s = _dot_nt(q[:, 0:LANES], tile(k_ref, 0))
                for c in range(1, nct):
                    s = s + _dot_nt(q[:, c * LANES:(c + 1) * LANES], tile(k_ref, c))
            else:
                s = _dot_nt(q, k_ref[g, :, cs].astype(bf16))
            s = s * scale
            s = s - jnp.max(s, axis=-1, keepdims=True)
            e = jnp.exp(s)
            p = (e / jnp.sum(e, axis=-1, keepdims=True)).astype(bf16)
            if interleaved:
                for c in range(nct):
                    o_ref[rs, h * X_HEAD_DIM + c * LANES:h * X_HEAD_DIM + (c + 1) * LANES] = _dot(p, tile(v_ref, c))
            else:
                o_ref[rs, cs] = _dot(p, v_ref[g, :, cs].astype(bf16))
        return carry

    lax.fori_loop(0, nseq, seq_body, 0)


def _attention(q, mk, mv, l, row0, nb, seq_len, nseq, rows, o_prev=None):
    r = nseq * rows
    steps_per_seq = seq_len // rows if nseq == 1 else 1
    rb0 = row0 // r
    interleaved = mk.ndim == 4
    if interleaved:
        kv_spec = pl.BlockSpec((None, nseq, mk.shape[2], LANES), lambda b, t: (l, b, 0, 0))
    else:
        kv_spec = pl.BlockSpec((nseq, MEM_LEN, D_MODEL), lambda b, t: (b, 0, 0))
    return pl.pallas_call(
        functools.partial(_attn_kernel, nseq=nseq, rows=rows, interleaved=interleaved),
        out_shape=jax.ShapeDtypeStruct(q.shape, f32),
        grid=(nb // nseq, steps_per_seq),
        in_specs=[
            pl.BlockSpec((r, D_MODEL), lambda b, t: (rb0 + b * steps_per_seq + t, 0)),
            kv_spec, kv_spec,
        ] + ([pl.BlockSpec(memory_space=pl.ANY)] if o_prev is not None else []),
        out_specs=pl.BlockSpec((r, D_MODEL), lambda b, t: (rb0 + b * steps_per_seq + t, 0)),
        input_output_aliases=({3: 0} if o_prev is not None else {}),
        compiler_params=_cp("parallel", "arbitrary"),
        name="mem_attn",
    )(*([q, mk, mv] + ([o_prev] if o_prev is not None else [])))


ROW_BLOCK = 128


def _cumsum_rows(tri_b, x):
    hi = x.astype(bf16)
    r1 = x - hi.astype(f32)
    mid = r1.astype(bf16)
    lo = (r1 - mid.astype(f32)).astype(bf16)
    return _dot(tri_b, hi) + _dot(tri_b, mid) + _dot(tri_b, lo)


def _chunk_tril(rb, cs):
    r = lax.broadcasted_iota(jnp.int32, (rb, rb), 0)
    c = lax.broadcasted_iota(jnp.int32, (rb, rb), 1)
    return (r >= c) & ((r ^ c) < cs)


def _rows_of(x, first, width):
    parts = [jnp.broadcast_to(x[j * width + first:j * width + first + 1, :], (width, x.shape[1]))
             for j in range(x.shape[0] // width)]
    return parts[0] if len(parts) == 1 else jnp.concatenate(parts, axis=0)


def _ssd_kernel(*refs, cs, nblk, per_seq_state, n_alias):
    (z_ref, xs_ref, bm_ref, cm_ref, dt_ref, cx0_ref, cb0_ref, cc0_ref, h0_ref,
     wx_ref, wb_ref, wc_ref, bx_ref, bb_ref, bc_ref, hp_ref, nrm_ref) = refs[:17]
    (y_ref, cxo_ref, cbo_ref, cco_ref, ho_ref,
     hx_ref, hb_ref, hc_ref, yg_ref, yi_ref, xw_ref, bs_ref, cs_ref, el_ref) = refs[17 + n_alias:]
    rb = ROW_BLOCK
    spb = rb // cs

    @pl.when(pl.program_id(2) == 0)
    def _():
        ho_ref[...] = h0_ref[...]
        cxo_ref[...] = cx0_ref[...]
        cbo_ref[...] = cb0_ref[...]
        cco_ref[...] = cc0_ref[...]

    hx_ref[...] = jnp.zeros_like(hx_ref)
    hb_ref[...] = jnp.zeros_like(hb_ref)
    hc_ref[...] = jnp.zeros_like(hc_ref)

    tri = _chunk_tril(rb, cs)
    tri_b = tri.astype(bf16)
    lane = lax.broadcasted_iota(jnp.int32, (rb, LANES), 1)
    left = lane < SSD_HEAD_DIM
    top = lax.broadcasted_iota(jnp.int32, (LANES, LANES), 0) < SSD_HEAD_DIM
    hp = hp_ref[...]
    a_neg = -jnp.exp(hp[1:2, :])
    npair = SSD_HEADS_PER_GROUP // 2
    nhist = SSD_CONV - 1

    def block_body(blk, carry):
        r0 = pl.multiple_of(blk * rb, rb)
        rs = pl.ds(r0, rb)

        def slot_of(j):
            return blk * spb + j if per_seq_state else 0

        def conv(h_ref, st_ref, u_ref, w_ref, b_ref):
            u = u_ref[rs, :]
            for j in range(spb):
                h_ref[j * cs:j * cs + nhist, :] = st_ref[slot_of(j)]
            hist = h_ref[...]
            pos = lax.broadcasted_iota(jnp.int32, u.shape, 0) & (cs - 1)
            out = b_ref[...] + u * w_ref[nhist:nhist + 1, :]
            for r in range(1, SSD_CONV):
                up = (rb - (nhist - r)) % rb
                older = jnp.where(pos >= r, pltpu.roll(u, r, axis=0), pltpu.roll(hist, up, axis=0) if up else hist)
                out = out + older * w_ref[nhist - r:nhist - r + 1, :]
            for j in range(spb):
                st_ref[slot_of(j)] = u[j * cs + cs - nhist:j * cs + cs, :]
            return _silu(out)

        xc = conv(hx_ref, cxo_ref, xs_ref, wx_ref, bx_ref)
        bmat = conv(hb_ref, cbo_ref, bm_ref, wb_ref, bb_ref)
        cmat = conv(hc_ref, cco_ref, cm_ref, wc_ref, bc_ref)
        bs_ref[...] = bmat
        cs_ref[...] = cmat
        bmb = bmat.astype(bf16)
        cmb = cmat.astype(bf16)

        dt = _softplus(dt_ref[rs, :] + hp[0:1, :])
        acum = _cumsum_rows(tri_b, dt * a_neg)
        acum_t = acum.T
        dt_t = dt.T
        a_last = _rows_of(acum, cs - 1, cs)
        ea = jnp.exp(acum)
        tail = jnp.exp(a_last - acum) * dt
        el_ref[...] = jnp.exp(a_last)
        cb = _dot_nt(cmb, bmb)

        def pair_lanes(x, pp):
            return jnp.where(left, x[:, 2 * pp:2 * pp + 1], x[:, 2 * pp + 1:2 * pp + 2])

        for pp in range(npair):
            ls = slice(pp * LANES, (pp + 1) * LANES)
            xp = xc[:, ls]
            xpb = xp.astype(bf16)

            def head_w(j):
                seg = acum[:, j:j + 1] - acum_t[j:j + 1, :]
                dec = jnp.exp(jnp.where(tri, seg, -jnp.inf))
                return (cb * dec * dt_t[j:j + 1, :]).astype(bf16)

            zero = jnp.zeros_like(xpb)
            y = _dot(head_w(2 * pp), jnp.where(left, xpb, zero)) + _dot(head_w(2 * pp + 1), jnp.where(left, zero, xpb))
            yg_ref[:, ls] = y + xp * pair_lanes(hp[2:3, :], pp)
            xw_ref[:, ls] = xp * pair_lanes(tail, pp)

        def seq_body(j, c):
            slot = slot_of(j)
            rl = pl.ds(pl.multiple_of(j * cs, 8), cs)
            cj = cs_ref[rl, :].astype(bf16)
            bj = bs_ref[rl, :].astype(bf16)
            el = el_ref[pl.ds(j * cs, 1), :]
            for pp in range(npair):
                ls = slice(pp * LANES, (pp + 1) * LANES)
                hstate = ho_ref[slot, ls, :]
                yi_ref[rl, ls] = _dot_nt(cj, hstate.astype(bf16))
                dec_rows = jnp.where(top, el[:, 2 * pp:2 * pp + 1], el[:, 2 * pp + 1:2 * pp + 2])
                ho_ref[slot, ls, :] = hstate * dec_rows + _dot_tn(xw_ref[rl, ls].astype(bf16), bj)
            return c

        lax.fori_loop(0, spb, seq_body, 0, unroll=min(spb, 2))

        for pp in range(npair):
            ls = slice(pp * LANES, (pp + 1) * LANES)
            yg_ref[:, ls] = (yg_ref[:, ls] + yi_ref[:, ls] * pair_lanes(ea, pp)) * _silu(z_ref[rs, ls])
        y_ref[rs, :] = _rms_norm(yg_ref[...], nrm_ref[...]).astype(y_ref.dtype)
        return carry

    lax.fori_loop(0, nblk, block_body, 0, unroll=2 if nblk % 2 == 0 else 1)


def _ssd(hid, conv_st, h_st, l, st_l, conv_w, conv_b, hp, nrm, prev, y_prev, *, row0, nb, seq_len, cs, nblk):
    rows = nblk * ROW_BLOCK
    per_seq_state = seq_len == cs
    g = rows // seq_len if per_seq_state else 1
    nt = 1 if per_seq_state else seq_len // rows
    rb0 = row0 // rows
    gw = SSD_GROUP_W
    extra = [a for a in (prev, y_prev) if a is not None]

    def rowblk(b, t):
        return rb0 + b * nt + t

    def hid_spec(col0, width):
        return pl.BlockSpec((rows, width), lambda b, gi, t: (rowblk(b, t), col0 // width + gi))

    def st_spec(col0, width):
        return pl.BlockSpec((None, g, 3, width), lambda b, gi, t: (st_l, b, 0, col0 // width + gi))

    def w_spec(nrow, col0, width):
        return pl.BlockSpec((None, nrow, width), lambda b, gi, t: (l, 0, col0 // width + gi))

    def sto_spec(width):
        return pl.BlockSpec((g, 3, width), lambda b, gi, t: (b, 0, gi))

    kern = functools.partial(_ssd_kernel, cs=cs, nblk=nblk, per_seq_state=per_seq_state, n_alias=len(extra))
    args = [hid, hid, hid, hid, hid, conv_st, conv_st, conv_st, h_st,
            conv_w, conv_w, conv_w, conv_b, conv_b, conv_b, hp, nrm] + extra
    aliases = {}
    if prev is not None:
        aliases[17] = 4
    if y_prev is not None:
        aliases[len(args) - 1] = 0
    nbc = SSD_GROUPS * SSD_STATE
    rbk = ROW_BLOCK
    return pl.pallas_call(
        kern,
        out_shape=(jax.ShapeDtypeStruct((hid.shape[0], MIX_W), f32),
                   jax.ShapeDtypeStruct((nb, 3, MIX_W), f32),
                   jax.ShapeDtypeStruct((nb, 3, nbc), f32),
                   jax.ShapeDtypeStruct((nb, 3, nbc), f32),
                   jax.ShapeDtypeStruct((DEPTH, nb, SSD_GROUPS, gw, SSD_STATE), f32)),
        grid=(nb // g, SSD_GROUPS, nt),
        input_output_aliases=aliases,
        in_specs=[
            hid_spec(COL_Z, gw), hid_spec(COL_XS, gw), hid_spec(COL_BM, LANES), hid_spec(COL_CM, LANES),
            hid_spec(COL_DT, LANES),
            st_spec(0, gw), st_spec(MIX_W, LANES), st_spec(MIX_W + nbc, LANES),
            pl.BlockSpec((None, g, None, gw, SSD_STATE), lambda b, gi, t: (st_l, b, gi, 0, 0)),
            w_spec(SSD_CONV, 0, gw), w_spec(SSD_CONV, MIX_W, LANES), w_spec(SSD_CONV, MIX_W + nbc, LANES),
            w_spec(1, 0, gw), w_spec(1, MIX_W, LANES), w_spec(1, MIX_W + nbc, LANES),
            pl.BlockSpec((None, None, 8, LANES), lambda b, gi, t: (l, gi, 0, 0)),
            pl.BlockSpec((None, 1, gw), lambda b, gi, t: (l, 0, gi)),
        ] + [pl.BlockSpec(memory_space=pl.ANY)] * len(extra),
        out_specs=(
            pl.BlockSpec((rows, gw), lambda b, gi, t: (rowblk(b, t), gi)),
            sto_spec(gw), sto_spec(LANES), sto_spec(LANES),
            pl.BlockSpec((None, g, None, gw, SSD_STATE), lambda b, gi, t: (l, b, gi, 0, 0)),
        ),
        scratch_shapes=[pltpu.VMEM((rbk, gw), f32), pltpu.VMEM((rbk, LANES), f32), pltpu.VMEM((rbk, LANES), f32),
                        pltpu.VMEM((rbk, gw), f32), pltpu.VMEM((rbk, gw), f32), pltpu.VMEM((rbk, gw), f32),
                        pltpu.VMEM((rbk, LANES), f32), pltpu.VMEM((rbk, LANES), f32), pltpu.VMEM((rbk, LANES), f32)],
        compiler_params=_cp("parallel", "parallel", "arbitrary"),
        name="ssd_scan",
    )(*args)


def _mlstm_kernel(*refs, cs, nblk, per_seq_state, n_alias):
    q_ref, k_ref, v_ref, o_ref, gt_ref, c0_ref, n0_ref, m0_ref, gb_ref, nrm_ref = refs[:10]
    y_ref, co_ref, no_ref, mo_ref, num_ref, kw_ref, dec_ref, mt_ref, sp_ref = refs[10 + n_alias:]
    rb = ROW_BLOCK
    spb = rb // cs

    @pl.when(pl.program_id(2) == 0)
    def _():
        co_ref[...] = c0_ref[...]
        no_ref[...] = n0_ref[...]
        mo_ref[...] = m0_ref[...]

    tri = _chunk_tril(rb, cs)
    tri_b = tri.astype(bf16)
    kscale = ML_DK ** -0.5

    def block_body(blk, carry):
        r0 = pl.multiple_of(blk * rb, rb)
        rs = pl.ds(r0, rb)

        def slot_of(j):
            return blk * spb + j if per_seq_state else 0

        def per_seq(ref, width):
            parts = [jnp.broadcast_to(ref[slot_of(j)], (cs, width)) for j in range(spb)]
            return parts[0] if spb == 1 else jnp.concatenate(parts, axis=0)

        gates = gt_ref[rs, :] + gb_ref[...]
        bfull = _cumsum_rows(tri_b, _log_sigmoid(gates))
        logi_c = gates[:, 0:1]
        b_c = bfull[:, 1:2]
        logi_r = gates.T[0:1, :]
        b_r = bfull.T[1:2, :]
        m_prev = per_seq(mo_ref, 1)
        n_prev = per_seq(no_ref, ML_DK)

        dmat = jnp.where(tri, b_c - b_r + logi_r, -jnp.inf)
        prev = b_c + m_prev
        mt = jnp.maximum(prev, jnp.max(dmat, axis=-1, keepdims=True))
        wts = jnp.exp(dmat - mt)
        sprev = jnp.exp(prev - mt)

        qh = q_ref[rs, :]
        kh = k_ref[rs, :] * kscale
        qk = _dot_nt(qh.astype(bf16), kh.astype(bf16)) * wts
        num_ref[...] = _dot(qk.astype(bf16), v_ref[rs, :].astype(bf16))
        den = jnp.sum(qk, axis=-1, keepdims=True) + jnp.sum(qh * n_prev, axis=-1, keepdims=True) * sprev

        m_new = _rows_of(mt, cs - 1, cs)
        b_last = _rows_of(b_c, cs - 1, cs)
        kw_ref[...] = kh * jnp.exp(b_last - b_c + logi_c - m_new)
        dec_ref[...] = jnp.broadcast_to(jnp.exp(b_last + m_prev - m_new), (rb, LANES))
        mt_ref[...] = jnp.broadcast_to(m_new, (rb, LANES))
        sp_ref[...] = jnp.broadcast_to(sprev, (rb, LANES))

        def seq_body(j, c):
            slot = slot_of(j)
            rl = pl.ds(pl.multiple_of(j * cs, 8), cs)
            rg = pl.ds(pl.multiple_of(r0 + j * cs, 8), cs)
            cst = co_ref[slot]
            num_ref[rl, :] += _dot(q_ref[rg, :].astype(bf16), cst.astype(bf16)) * sp_ref[rl, 0:1]
            kw = kw_ref[rl, :]
            dec = dec_ref[pl.ds(j * cs, 1), 0:1]
            co_ref[slot] = cst * dec + _dot_tn(kw.astype(bf16), v_ref[rg, :].astype(bf16))
            no_ref[slot] = no_ref[slot] * dec + jnp.sum(kw, axis=0, keepdims=True)
            mo_ref[slot] = mt_ref[pl.ds(j * cs, 1), 0:1]
            return c

        lax.fori_loop(0, spb, seq_body, 0, unroll=min(spb, 2))

        hh = num_ref[...] / jnp.maximum(jnp.abs(den), jnp.exp(-mt))
        y = _rms_norm(hh, nrm_ref[...]) * _sigmoid(o_ref[rs, :])
        y_ref[rs, :] = y.astype(y_ref.dtype)
        return carry

    lax.fori_loop(0, nblk, block_body, 0, unroll=2 if nblk % 2 == 0 else 1)


def _mlstm(hid, c_st, n_st, m_st, l, st_l, gbias, nrm, prev, y_prev, *, row0, nb, seq_len, cs, nblk):
    rows = nblk * ROW_BLOCK
    per_seq_state = seq_len == cs
    g = rows // seq_len if per_seq_state else 1
    nt = 1 if per_seq_state else seq_len // rows
    rb0 = row0 // rows
    extra = [a for a in (prev, y_prev) if a is not None]

    def hid_spec(col0, width):
        return pl.BlockSpec((rows, width), lambda b, h, t: (rb0 + b * nt + t, col0 // width + h))

    kern = functools.partial(_mlstm_kernel, cs=cs, nblk=nblk, per_seq_state=per_seq_state, n_alias=len(extra))
    args = [hid, hid, hid, hid, hid, c_st, n_st, m_st, gbias, nrm] + extra
    aliases = {}
    if prev is not None:
        aliases[10] = 1
    if y_prev is not None:
        aliases[len(args) - 1] = 0
    rbk = ROW_BLOCK
    return pl.pallas_call(
        kern,
        input_output_aliases=aliases,
        out_shape=(jax.ShapeDtypeStruct((hid.shape[0], MIX_W), f32),
                   jax.ShapeDtypeStruct((DEPTH, nb, ML_HEADS, ML_DK, ML_DV), f32),
                   jax.ShapeDtypeStruct((nb, ML_HEADS, 1, ML_DK), f32),
                   jax.ShapeDtypeStruct((nb, ML_HEADS, 1, 1), f32)),
        grid=(nb // g, ML_HEADS, nt),
        in_specs=[
            hid_spec(COL_MQ, ML_DK), hid_spec(COL_MK, ML_DK), hid_spec(COL_MV, ML_DV), hid_spec(COL_MO, ML_DV),
            hid_spec(COL_MG, LANES),
            pl.BlockSpec((None, g, None, ML_DK, ML_DV), lambda b, h, t: (st_l, b, h, 0, 0)),
            pl.BlockSpec((None, g, None, 1, ML_DK), lambda b, h, t: (st_l, b, h, 0, 0)),
            pl.BlockSpec((None, g, None, 1, 1), lambda b, h, t: (st_l, b, h, 0, 0)),
            pl.BlockSpec((None, None, 1, LANES), lambda b, h, t: (l, h, 0, 0)),
            pl.BlockSpec((None, 1, ML_DV), lambda b, h, t: (l, 0, h)),
        ] + [pl.BlockSpec(memory_space=pl.ANY)] * len(extra),
        out_specs=(
            pl.BlockSpec((rows, ML_DV), lambda b, h, t: (rb0 + b * nt + t, h)),
            pl.BlockSpec((None, g, None, ML_DK, ML_DV), lambda b, h, t: (l, b, h, 0, 0)),
            pl.BlockSpec((g, None, 1, ML_DK), lambda b, h, t: (b, h, 0, 0)),
            pl.BlockSpec((g, None, 1, 1), lambda b, h, t: (b, h, 0, 0)),
        ),
        scratch_shapes=[pltpu.VMEM((rbk, ML_DV), f32), pltpu.VMEM((rbk, ML_DK), f32), pltpu.VMEM((rbk, LANES), f32),
                        pltpu.VMEM((rbk, LANES), f32), pltpu.VMEM((rbk, LANES), f32)],
        compiler_params=_cp("parallel", "parallel", "arbitrary"),
        name="mlstm_scan",
    )(*args)


HG_ROWS = ROW_BLOCK
HG_TILE = 8


def _hgrn_kernel(*refs, cs, nblk, per_seq_state, layer, n_alias):
    q_ref, f_ref, i_ref, g_ref, s0_ref, lg_ref, nrm_ref = refs[:7]
    y_ref, so_ref, b_ref, qe_ref, kd_ref, v_ref, yi_ref = refs[7 + n_alias:]
    rb = HG_ROWS
    spb = rb // cs

    @pl.when(pl.program_id(2) == 0)
    def _():
        so_ref[...] = s0_ref[...]

    lg = lg_ref[...]
    pe = jnp.exp(lg - jnp.max(lg, axis=0, keepdims=True))
    pw = pe / jnp.sum(pe, axis=0, keepdims=True)
    lb = jnp.zeros((1, HG_DK), f32)
    for j in range(1, layer + 1):
        lb = lb + pw[j:j + 1, :]
    log_lb = jnp.log(lb)
    log_1m = jnp.log1p(-lb)

    row_a = lax.broadcasted_iota(jnp.int32, (rb, rb), 0)
    lane_a = lax.broadcasted_iota(jnp.int32, (rb, rb), 1)
    row_c = lax.broadcasted_iota(jnp.int32, (rb, HG_DK), 0)
    tri_b = ((row_a >= lane_a) & ((row_a ^ lane_a) < cs)).astype(bf16)

    def block_body(blk, carry):
        rs = pl.ds(pl.multiple_of(blk * rb, rb), rb)
        fr = f_ref[rs, :]
        en = jnp.exp(-jnp.abs(fr))
        log_sig = jnp.minimum(fr, 0.0) - jnp.log1p(en)
        sig_neg = jnp.where(fr >= 0.0, en, 1.0) / (1.0 + en)
        if layer == 0:
            logf, kk = log_sig, sig_neg
        else:
            u = log_1m + log_sig
            mx = jnp.maximum(log_lb, u)
            mn = jnp.minimum(log_lb, u)
            logf = mx + jnp.log1p(jnp.exp(mn - mx))
            kk = (1.0 - lb) * sig_neg
        qq = _silu(q_ref[rs, :])
        v = i_ref[rs, :]
        b = _cumsum_rows(tri_b, logf)

        att = jnp.where(lane_a == row_a, jnp.sum(qq * kk, axis=-1, keepdims=True), 0.0)
        for r in range(1, HG_TILE):
            valid = (row_c & (HG_TILE - 1)) >= r
            e = jnp.exp(b - pltpu.roll(b, r, axis=0))
            col = jnp.sum(jnp.where(valid, qq * e * pltpu.roll(kk, r, axis=0), 0.0), axis=-1, keepdims=True)
            att = jnp.where(lane_a == row_a - r, col, att)
        size = cs
        while size > HG_TILE:
            half = size // 2
            e = jnp.exp(-jnp.abs(b - _rows_of(b, half - 1, size)))
            p = _dot_nt((qq * e).astype(bf16), (kk * e).astype(bf16))
            split = (row_a ^ lane_a)
            own = (split >= half) & (split < size) & ((row_a & half) != 0)
            att = jnp.where(own, p, att)
            size = half

        yi_ref[...] = _dot(att.astype(bf16), v.astype(bf16))
        b_ref[...] = b
        qe_ref[...] = qq * jnp.exp(b)
        kd_ref[...] = kk * jnp.exp(_rows_of(b, cs - 1, cs) - b)
        v_ref[...] = v

        def seq_body(j, c):
            slot = blk * spb + j if per_seq_state else 0
            r = pl.ds(pl.multiple_of(j * cs, 8), cs)
            st = so_ref[slot]
            yi_ref[r, :] += _dot(qe_ref[r, :].astype(bf16), st.astype(bf16))
            b_last = b_ref[pl.ds(j * cs + cs - 1, 1), :]
            dcol = jnp.exp(jnp.broadcast_to(b_last, (8, HG_DK))).T[:, 0:1]
            so_ref[slot] = st * dcol + _dot_tn(kd_ref[r, :].astype(bf16), v_ref[r, :].astype(bf16))
            return c

        lax.fori_loop(0, spb, seq_body, 0, unroll=min(spb, 2))
        y = _rms_norm(yi_ref[...], nrm_ref[...]) * _sigmoid(g_ref[rs, :])
        y_ref[rs, :] = y.astype(y_ref.dtype)
        return carry

    lax.fori_loop(0, nblk, block_body, 0, unroll=2 if nblk % 2 == 0 else 1)


def _hgrn(hid, s_st, l, st_l, logits, nrm, prev, y_prev, *, row0, nb, seq_len, cs, nblk):
    rows = nblk * HG_ROWS
    per_seq_state = seq_len == cs
    g = rows // seq_len if per_seq_state else 1
    nt = 1 if per_seq_state else seq_len // rows
    rb0 = row0 // rows
    extra = [a for a in (prev, y_prev) if a is not None]

    def hid_spec(col0):
        return pl.BlockSpec((rows, LANES), lambda b, h, t: (rb0 + b * nt + t, col0 // LANES + h))

    kern = functools.partial(_hgrn_kernel, cs=cs, nblk=nblk, per_seq_state=per_seq_state, layer=l, n_alias=len(extra))
    in_specs = [
        hid_spec(COL_HQ), hid_spec(COL_HF), hid_spec(COL_HI), hid_spec(COL_HG),
        pl.BlockSpec((None, g, None, HG_DK, HG_DV), lambda b, h, t: (st_l, b, h, 0, 0)),
        pl.BlockSpec((DEPTH, LANES), lambda b, h, t: (0, h)),
        pl.BlockSpec((None, 1, LANES), lambda b, h, t: (l, 0, h)),
    ]
    args = [hid, hid, hid, hid, s_st, logits, nrm] + extra
    in_specs += [pl.BlockSpec(memory_space=pl.ANY)] * len(extra)
    aliases = {}
    if prev is not None:
        aliases[7] = 1
    if y_prev is not None:
        aliases[len(args) - 1] = 0
    return pl.pallas_call(
        kern,
        out_shape=(jax.ShapeDtypeStruct((hid.shape[0], MIX_W), f32),
                   jax.ShapeDtypeStruct((DEPTH, nb, HG_HEADS, HG_DK, HG_DV), f32)),
        grid=(nb // g, HG_HEADS, nt),
        in_specs=in_specs,
        out_specs=(
            pl.BlockSpec((rows, LANES), lambda b, h, t: (rb0 + b * nt + t, h)),
            pl.BlockSpec((None, g, None, HG_DK, HG_DV), lambda b, h, t: (l, b, h, 0, 0)),
        ),
        scratch_shapes=[pltpu.VMEM((HG_ROWS, HG_DK), f32)] * 5,
        input_output_aliases=aliases,
        compiler_params=_cp("parallel", "parallel", "arbitrary"),
        name="hgrn_scan",
    )(*args)


def _prep_w_in(w_in):
    off = {}
    start = 0
    names = ("z", "xbc", "dt", "mq", "mk", "mv", "mo", "mi", "mf", "hq", "hf", "hi", "hg", "gate")
    sizes = (MIX_W, MIX_W + 2 * SSD_GROUPS * SSD_STATE, SSD_HEADS, ML_HEADS * ML_DK, ML_HEADS * ML_DK, MIX_W, MIX_W,
             ML_HEADS, ML_HEADS, MIX_W, MIX_W, MIX_W, MIX_W, 3 * D_MODEL)
    for nme, sz in zip(names, sizes):
        off[nme] = (start, sz)
        start += sz

    def seg(nme, lo=0, hi=None):
        s, sz = off[nme]
        hi = sz if hi is None else hi
        return w_in[:, :, s + lo:s + hi]

    nbc = SSD_GROUPS * SSD_STATE
    dpth, kdim = w_in.shape[0], w_in.shape[1]
    dt = seg("dt").reshape(dpth, kdim, SSD_GROUPS, SSD_HEADS_PER_GROUP)
    dt = jnp.pad(dt, ((0, 0), (0, 0), (0, 0), (0, LANES - SSD_HEADS_PER_GROUP))).reshape(dpth, kdim, SSD_GROUPS * LANES)
    mg = jnp.stack([seg("mi"), seg("mf")], axis=-1)
    mg = jnp.pad(mg, ((0, 0), (0, 0), (0, 0), (0, LANES - 2))).reshape(dpth, kdim, ML_HEADS * LANES)
    cols = [seg("z"), seg("xbc", 0, MIX_W), seg("mv"), seg("mo"), seg("hq"), seg("hf"), seg("hi"), seg("hg"),
            seg("gate"), seg("mq"), seg("mk"), seg("xbc", MIX_W, MIX_W + nbc), seg("xbc", MIX_W + nbc, MIX_W + 2 * nbc),
            dt, mg]
    return jnp.concatenate([c.astype(bf16) for c in cols], axis=-1)


def _row_major_groups(x_prompt, x_sample):
    return jnp.concatenate([x_prompt.reshape(-1, D_MODEL), x_sample.reshape(-1, D_MODEL)], axis=0)


def kernel(x_prompt, x_sample, mem_prompt, state_ssd_conv, state_ssd, state_mlstm_c, state_mlstm_n, state_mlstm_m, state_hgrn, cache_mem_k, cache_mem_v, ln_g, ln_b, ffn_w1, ffn_w3, ffn_w2, w_in, ssd_conv_w, ssd_conv_b, ssd_dt_bias, ssd_a_log, ssd_d, ssd_norm, ml_gate_bias, ml_norm, hg_lb_logits, hg_norm, w_branch, w_mix_out, x_wq, x_wk, x_wv, x_wo):
    nbp, lp = x_prompt.shape[0], x_prompt.shape[1]
    nbs, ls = x_sample.shape[0], x_sample.shape[1]
    mp = nbp * lp
    depth = w_in.shape[0]

    w1b = ffn_w1.astype(bf16)
    w3b = ffn_w3.astype(bf16)
    w2b = ffn_w2.astype(bf16)
    w_hid = _prep_w_in(w_in)
    wb = w_branch.astype(bf16)
    w_mo = w_mix_out.astype(bf16)
    w_q = x_wq.astype(bf16)
    w_o = x_wo.astype(bf16)
    w_kv = jnp.concatenate([x_wk, x_wv], axis=-1).astype(bf16)
    lng = ln_g.reshape(depth, 4, 1, D_MODEL)
    lnb = ln_b.reshape(depth, 4, 1, D_MODEL)

    def per_head_rows(*rows):
        r = jnp.stack([a.reshape(depth, SSD_GROUPS, SSD_HEADS_PER_GROUP) for a in rows], axis=2)
        return jnp.pad(r, ((0, 0), (0, 0), (0, 8 - len(rows)), (0, LANES - SSD_HEADS_PER_GROUP)))

    ssd_hp = per_head_rows(ssd_dt_bias, ssd_a_log, ssd_d)
    ssd_nrm = ssd_norm.reshape(depth, 1, MIX_W)
    conv_b = ssd_conv_b.reshape(depth, 1, -1)
    ml_gb = jnp.pad(jnp.swapaxes(ml_gate_bias, 1, 2), ((0, 0), (0, 0), (0, LANES - 2))).reshape(depth, ML_HEADS, 1, LANES)
    ml_nrm = ml_norm.reshape(depth, 1, MIX_W)
    hg_nrm = hg_norm.reshape(depth, 1, MIX_W)

    s_ssd = state_ssd.reshape(depth, nbs, SSD_GROUPS, SSD_GROUP_W, SSD_STATE)
    s_n = state_mlstm_n.reshape(depth, nbs, ML_HEADS, 1, ML_DK)
    s_m = state_mlstm_m.reshape(depth, nbs, ML_HEADS, 1, 1)
    nct = X_HEAD_DIM // LANES

    def cache_rows(c):
        c = c.reshape(depth, nbs, MEM_LEN, X_HEADS, nct, LANES)
        return jnp.swapaxes(c, 3, 4).reshape(depth, nbs, MEM_LEN * nct * X_HEADS, LANES)

    cache_k = cache_rows(cache_mem_k)
    cache_v = cache_rows(cache_mem_v)
    z_conv = jnp.zeros((1, nbp, SSD_CONV - 1, state_ssd_conv.shape[-1]), f32)
    z_ssd = jnp.zeros((1, nbp, SSD_GROUPS, SSD_GROUP_W, SSD_STATE), f32)
    z_c = jnp.zeros((1, nbp, ML_HEADS, ML_DK, ML_DV), f32)
    z_n = jnp.zeros((1, nbp, ML_HEADS, 1, ML_DK), f32)
    z_m = jnp.zeros((1, nbp, ML_HEADS, 1, 1), f32)
    z_hg = jnp.zeros((1, nbp, HG_HEADS, HG_DK, HG_DV), f32)

    mem_rows = mem_prompt.reshape(nbp * MEM_LEN, D_MODEL)
    x = _row_major_groups(x_prompt, x_sample)

    p_small = [[] for _ in range(5)]
    s_small = [[] for _ in range(3)]
    p_ssd = p_c = p_hg = s_ssd_o = s_c_o = s_hg_o = None
    pc, sc = 128, ls
    blocks_per_seq = lp // ROW_BLOCK
    pb = min(8, blocks_per_seq)
    pb_hg = min(16, blocks_per_seq)
    sb_hg = min(4, nbs * ls // ROW_BLOCK)

    for l in range(depth):
        x = _ffn_ln(x, w1b, w3b, w2b, lng, lnb, l, 0, 0)
        hid = _matmul(x, w_hid, l, f32, tm=1024, tn=1024, name="in_proj")

        yp, cx, cb_, cc, p_ssd = _ssd(hid, z_conv, z_ssd, l, 0, ssd_conv_w, conv_b, ssd_hp, ssd_nrm, p_ssd, None,
                                      row0=0, nb=nbp, seq_len=lp, cs=pc, nblk=pb)
        y_ssd, sx, sb, scc, s_ssd_o = _ssd(hid, state_ssd_conv, s_ssd, l, l, ssd_conv_w, conv_b, ssd_hp, ssd_nrm, s_ssd_o,
                                           yp, row0=mp, nb=nbs, seq_len=ls, cs=sc, nblk=1)
        p_small[0].append(jnp.concatenate([cx, cb_, cc], axis=-1))
        s_small[0].append(jnp.concatenate([sx, sb, scc], axis=-1))

        yp, p_c, n1, m1 = _mlstm(hid, z_c, z_n, z_m, l, 0, ml_gb, ml_nrm, p_c, None,
                                 row0=0, nb=nbp, seq_len=lp, cs=pc, nblk=pb)
        y_ml, s_c_o, sn1, sm1 = _mlstm(hid, state_mlstm_c, s_n, s_m, l, l, ml_gb, ml_nrm, s_c_o, yp,
                                       row0=mp, nb=nbs, seq_len=ls, cs=sc, nblk=1)
        p_small[1].append(n1.reshape(nbp, ML_HEADS, ML_DK))
        p_small[2].append(m1.reshape(nbp, ML_HEADS))
        s_small[1].append(sn1.reshape(nbs, ML_HEADS, ML_DK))
        s_small[2].append(sm1.reshape(nbs, ML_HEADS))

        yp, p_hg = _hgrn(hid, z_hg, l, 0, hg_lb_logits, hg_nrm, p_hg, None,
                         row0=0, nb=nbp, seq_len=lp, cs=pc, nblk=pb_hg)
        y_hg, s_hg_o = _hgrn(hid, state_hgrn, l, l, hg_lb_logits, hg_nrm, s_hg_o, yp,
                             row0=mp, nb=nbs, seq_len=ls, cs=sc, nblk=sb_hg)

        s_mix = _merge((y_ssd, y_ml, y_hg), hid, wb, l)
        x = _res_ln(x, s_mix, w_mo, lng, lnb, l, 1)

        kv = _matmul(mem_rows, w_kv, l, f32, tm=nbp * MEM_LEN, name="mem_kv")
        pk = kv[:, :D_MODEL].reshape(nbp, MEM_LEN, D_MODEL)
        pv = kv[:, D_MODEL:].reshape(nbp, MEM_LEN, D_MODEL)
        p_small[3].append(pk.reshape(nbp, MEM_LEN, X_HEADS, X_HEAD_DIM))
        p_small[4].append(pv.reshape(nbp, MEM_LEN, X_HEADS, X_HEAD_DIM))

        q = _matmul(x, w_q, l, f32, tm=1024, name="q_proj")
        op = _attention(q, pk, pv, l, 0, nbp, lp, nseq=1, rows=512)
        o = _attention(q, cache_k, cache_v, l, mp, nbs, ls, nseq=4, rows=ls, o_prev=op)
        x = _res_ln(x, o, w_o, lng, lnb, l, 2)
        x = _ffn_ln(x, w1b, w3b, w2b, lng, lnb, l, 1, 3)

    y_prompt = x[:mp].reshape(nbp, lp, D_MODEL)
    y_sample = x[mp:].reshape(nbs, ls, D_MODEL)
    st = jnp.stack
    return (y_prompt, y_sample,
            st(p_small[0]), p_ssd.reshape(depth, nbp, SSD_HEADS, SSD_HEAD_DIM, SSD_STATE), p_c, st(p_small[1]),
            st(p_small[2]), p_hg, st(p_small[3]), st(p_small[4]),
            st(s_small[0]), s_ssd_o.reshape(depth, nbs, SSD_HEADS, SSD_HEAD_DIM, SSD_STATE), s_c_o, st(s_small[1]),
            st(s_small[2]), s_hg_o)
```

```python
import functools

import jax
import jax.numpy as jnp
from jax import lax
from jax.experimental import pallas as pl
from jax.experimental.pallas import tpu as pltpu

f32 = jnp.float32
bf16 = jnp.bfloat16

D_MODEL = 2048
DEPTH = 2
MIX_W = D_MODEL
SSD_HEAD_DIM = 64
SSD_HEADS = 32
SSD_GROUPS = 4
SSD_STATE = 128
SSD_CONV = 4
SSD_GROUP_W = MIX_W // SSD_GROUPS
SSD_HEADS_PER_GROUP = SSD_HEADS // SSD_GROUPS
ML_HEADS = 4
ML_DV = 512
ML_DK = 256
HG_HEADS = 16
HG_DK = 128
HG_DV = 128
MEM_LEN = 256
X_HEADS = 4
X_HEAD_DIM = 512
D_FF = 5504
FF_TILE = 512
FFN_ROWS = 1024
DN_ALPHA = (2.0 * DEPTH) ** 0.25
LN_EPS = 1e-5
RMS_EPS = 1e-6

LANES = 128
VMEM_LIMIT = 56 * 1024 * 1024

COL_Z = 0
COL_XS = 2048
COL_MV = 4096
COL_MO = 6144
COL_HQ = 8192
COL_HF = 10240
COL_HI = 12288
COL_HG = 14336
COL_GATE = 16384
COL_MQ = 22528
COL_MK = 23552
COL_BM = 24576
COL_CM = 25088
COL_DT = 25600
COL_MG = 26112
N_HID = 26624


def _cp(*sem):
    return pltpu.CompilerParams(dimension_semantics=sem, vmem_limit_bytes=VMEM_LIMIT)


def _dot(a, b):
    return jnp.dot(a, b, preferred_element_type=f32)


def _dot_nt(a, b):
    return lax.dot_general(a, b, (((1,), (1,)), ((), ())), preferred_element_type=f32)


def _dot_tn(a, b):
    return lax.dot_general(a, b, (((0,), (0,)), ((), ())), preferred_element_type=f32)


def _sigmoid(x):
    return 1.0 / (1.0 + jnp.exp(-x))


def _silu(x):
    return x * _sigmoid(x)


def _softplus(x):
    return jnp.maximum(x, 0.0) + jnp.log1p(jnp.exp(-jnp.abs(x)))


def _log_sigmoid(x):
    return jnp.minimum(x, 0.0) - jnp.log1p(jnp.exp(-jnp.abs(x)))


def _layer_norm(y, g, b):
    mu = jnp.mean(y, axis=-1, keepdims=True)
    yc = y - mu
    var = jnp.mean(yc * yc, axis=-1, keepdims=True)
    return yc * lax.rsqrt(var + LN_EPS) * g + b


def _rms_norm(y, g):
    return y * lax.rsqrt(jnp.mean(y * y, axis=-1, keepdims=True) + RMS_EPS) * g


def _row_tile(m, preferred):
    t = preferred
    while m % t:
        t //= 2
    return t


def _ffn_kernel(x_ref, w1_ref, w3_ref, w2_ref, g_ref, b_ref, o_ref, xb_ref, *, tf, d_ff):
    j = pl.program_id(1)
    last = pl.num_programs(1) - 1

    @pl.when(j == 0)
    def _():
        xb_ref[...] = x_ref[...].astype(bf16)
        o_ref[...] = jnp.zeros_like(o_ref)

    xb = xb_ref[...]
    h = _silu(_dot(xb, w1_ref[...])) * _dot(xb, w3_ref[...])

    if d_ff % tf == 0:
        o_ref[...] += _dot(h.astype(bf16), w2_ref[...])
    else:
        @pl.when(j < last)
        def _():
            o_ref[...] += _dot(h.astype(bf16), w2_ref[...])

        @pl.when(j == last)
        def _():
            valid = d_ff % tf
            w2 = w2_ref[...]
            hm = jnp.where(lax.broadcasted_iota(jnp.int32, (1, tf), 1) < valid, h, 0.0)
            w2m = jnp.where(lax.broadcasted_iota(jnp.int32, (tf, 1), 0) < valid, w2, jnp.zeros_like(w2))
            o_ref[...] += _dot(hm.astype(bf16), w2m)

    @pl.when(j == last)
    def _():
        y = DN_ALPHA * x_ref[...] + 0.5 * o_ref[...]
        o_ref[...] = _layer_norm(y, g_ref[...], b_ref[...])


def _ffn_ln(x, w1, w3, w2, ln_g, ln_b, l, k, ln_i, tm=FFN_ROWS, tf=FF_TILE):
    m = x.shape[0]
    tm = _row_tile(m, tm)
    d_ff = w1.shape[-1]
    return pl.pallas_call(
        functools.partial(_ffn_kernel, tf=tf, d_ff=d_ff),
        out_shape=jax.ShapeDtypeStruct((m, D_MODEL), f32),
        grid=(m // tm, pl.cdiv(d_ff, tf)),
        in_specs=[
            pl.BlockSpec((tm, D_MODEL), lambda i, j: (i, 0), pipeline_mode=pl.Buffered(1)),
            pl.BlockSpec((None, None, D_MODEL, tf), lambda i, j: (l, k, 0, j)),
            pl.BlockSpec((None, None, D_MODEL, tf), lambda i, j: (l, k, 0, j)),
            pl.BlockSpec((None, None, tf, D_MODEL), lambda i, j: (l, k, j, 0)),
            pl.BlockSpec((None, None, 1, D_MODEL), lambda i, j: (l, ln_i, 0, 0)),
            pl.BlockSpec((None, None, 1, D_MODEL), lambda i, j: (l, ln_i, 0, 0)),
        ],
        out_specs=pl.BlockSpec((tm, D_MODEL), lambda i, j: (i, 0)),
        scratch_shapes=[pltpu.VMEM((tm, D_MODEL), bf16)],
        compiler_params=_cp("parallel", "arbitrary"),
        name="ffn_ln",
    )(x, w1, w3, w2, ln_g, ln_b)


def _mm_kernel(x_ref, w_ref, o_ref, xb_ref):
    @pl.when(pl.program_id(1) == 0)
    def _():
        xb_ref[...] = x_ref[...].astype(bf16)

    o_ref[...] = _dot(xb_ref[...], w_ref[...]).astype(o_ref.dtype)


def _matmul(x, w, l, out_dtype, tm, tn=512, name="matmul"):
    m, kdim = x.shape
    n = w.shape[-1]
    tm = _row_tile(m, tm)
    return pl.pallas_call(
        _mm_kernel,
        out_shape=jax.ShapeDtypeStruct((m, n), out_dtype),
        grid=(m // tm, n // tn),
        in_specs=[
            pl.BlockSpec((tm, kdim), lambda i, j: (i, 0)),
            pl.BlockSpec((None, kdim, tn), lambda i, j: (l, 0, j)),
        ],
        out_specs=pl.BlockSpec((tm, tn), lambda i, j: (i, j)),
        scratch_shapes=[pltpu.VMEM((tm, kdim), bf16)],
        compiler_params=_cp("parallel", "arbitrary"),
        name=name,
    )(x, w)


def _res_ln_kernel(x_ref, y_ref, w_ref, g_ref, b_ref, o_ref):
    r = _dot(y_ref[...].astype(bf16), w_ref[...])
    o_ref[...] = _layer_norm(DN_ALPHA * x_ref[...] + r, g_ref[...], b_ref[...])


def _res_ln(x, y, w, ln_g, ln_b, l, ln_i, tm=512):
    m = x.shape[0]
    tm = _row_tile(m, tm)
    return pl.pallas_call(
        _res_ln_kernel,
        out_shape=jax.ShapeDtypeStruct((m, D_MODEL), f32),
        grid=(m // tm,),
        in_specs=[
            pl.BlockSpec((tm, D_MODEL), lambda i: (i, 0)),
            pl.BlockSpec((tm, D_MODEL), lambda i: (i, 0)),
            pl.BlockSpec((None, D_MODEL, D_MODEL), lambda i: (l, 0, 0)),
            pl.BlockSpec((None, None, 1, D_MODEL), lambda i: (l, ln_i, 0, 0)),
            pl.BlockSpec((None, None, 1, D_MODEL), lambda i: (l, ln_i, 0, 0)),
        ],
        out_specs=pl.BlockSpec((tm, D_MODEL), lambda i: (i, 0)),
        compiler_params=_cp("parallel"),
        name="res_ln",
    )(x, y, w, ln_g, ln_b)


def _merge_kernel(y0_ref, y1_ref, y2_ref, g0_ref, g1_ref, g2_ref, w_ref, o_ref):
    acc = _sigmoid(g0_ref[...]) * _dot(y0_ref[...].astype(bf16), w_ref[0])
    acc += _sigmoid(g1_ref[...]) * _dot(y1_ref[...].astype(bf16), w_ref[1])
    acc += _sigmoid(g2_ref[...]) * _dot(y2_ref[...].astype(bf16), w_ref[2])
    o_ref[...] = acc.astype(o_ref.dtype)


def _merge(ys, hid, wb, l, tm=512, tn=512):
    m = hid.shape[0]
    tm = _row_tile(m, tm)
    gb = COL_GATE // tn
    nb = D_MODEL // tn
    y_spec = pl.BlockSpec((tm, MIX_W), lambda i, j: (i, 0))
    return pl.pallas_call(
        _merge_kernel,
        out_shape=jax.ShapeDtypeStruct((m, D_MODEL), bf16),
        grid=(m // tm, D_MODEL // tn),
        in_specs=[
            y_spec, y_spec, y_spec,
            pl.BlockSpec((tm, tn), lambda i, j: (i, gb + j)),
            pl.BlockSpec((tm, tn), lambda i, j: (i, gb + nb + j)),
            pl.BlockSpec((tm, tn), lambda i, j: (i, gb + 2 * nb + j)),
            pl.BlockSpec((None, 3, MIX_W, tn), lambda i, j: (l, 0, 0, j)),
        ],
        out_specs=pl.BlockSpec((tm, tn), lambda i, j: (i, j)),
        compiler_params=_cp("parallel", "arbitrary"),
        name="merge",
    )(ys[0], ys[1], ys[2], hid, hid, hid, wb)


def _attn_kernel(q_ref, k_ref, v_ref, *rest, nseq, rows, interleaved):
    o_ref = rest[-1]
    scale = X_HEAD_DIM ** -0.5
    nct = X_HEAD_DIM // LANES

    def seq_body(g, carry):
        rs = pl.ds(pl.multiple_of(g * rows, 8), rows)
        for h in range(X_HEADS):
            cs = slice(h * X_HEAD_DIM, (h + 1) * X_HEAD_DIM)
            q = q_ref[rs, cs].astype(bf16)
            if interleaved:
                def tile(ref, c):
                    return ref[g, pl.ds(c * X_HEADS + h, MEM_LEN, stride=X_HEADS * nct), :].astype(bf16)
                s = _dot_nt(q[:, 0:LANES], tile(k_ref, 0))
                for c in range(1, nct):
                    s = s + _dot_nt(q[:, c * LANES:(c + 1) * LANES], tile(k_ref, c))
            else:
                s = _dot_nt(q, k_ref[g, :, cs].astype(bf16))
            s = s * scale
            s = s - jnp.max(s, axis=-1, keepdims=True)
            e = jnp.exp(s)
            p = (e / jnp.sum(e, axis=-1, keepdims=True)).astype(bf16)
            if interleaved:
                for c in range(nct):
                    o_ref[rs, h * X_HEAD_DIM + c * LANES:h * X_HEAD_DIM + (c + 1) * LANES] = _dot(p, tile(v_ref, c))
            else:
                o_ref[rs, cs] = _dot(p, v_ref[g, :, cs].astype(bf16))
        return carry

    lax.fori_loop(0, nseq, seq_body, 0)


def _attention(q, mk, mv, l, row0, nb, seq_len, nseq, rows, o_prev=None):
    r = nseq * rows
    steps_per_seq = seq_len // rows if nseq == 1 else 1
    rb0 = row0 // r
    interleaved = mk.ndim == 4
    if interleaved:
        kv_spec = pl.BlockSpec((None, nseq, mk.shape[2], LANES), lambda b, t: (l, b, 0, 0))
    else:
        kv_spec = pl.BlockSpec((nseq, MEM_LEN, D_MODEL), lambda b, t: (b, 0, 0))
    return pl.pallas_call(
        functools.partial(_attn_kernel, nseq=nseq, rows=rows, interleaved=interleaved),
        out_shape=jax.ShapeDtypeStruct(q.shape, f32),
        grid=(nb // nseq, steps_per_seq),
        in_specs=[
            pl.BlockSpec((r, D_MODEL), lambda b, t: (rb0 + b * steps_per_seq + t, 0)),
            kv_spec, kv_spec,
        ] + ([pl.BlockSpec(memory_space=pl.ANY)] if o_prev is not None else []),
        out_specs=pl.BlockSpec((r, D_MODEL), lambda b, t: (rb0 + b * steps_per_seq + t, 0)),
        input_output_aliases=({3: 0} if o_prev is not None else {}),
        compiler_params=_cp("parallel", "arbitrary"),
        name="mem_attn",
    )(*([q, mk, mv] + ([o_prev] if o_prev is not None else [])))


ROW_BLOCK = 128


def _cumsum_rows(tri_b, x):
    hi = x.astype(bf16)
    r1 = x - hi.astype(f32)
    mid = r1.astype(bf16)
    lo = (r1 - mid.astype(f32)).astype(bf16)
    return _dot(tri_b, hi) + _dot(tri_b, mid) + _dot(tri_b, lo)


def _chunk_tril(rb, cs):
    r = lax.broadcasted_iota(jnp.int32, (rb, rb), 0)
    c = lax.broadcasted_iota(jnp.int32, (rb, rb), 1)
    return (r >= c) & ((r ^ c) < cs)


def _rows_of(x, first, width):
    parts = [jnp.broadcast_to(x[j * width + first:j * width + first + 1, :], (width, x.shape[1]))
             for j in range(x.shape[0] // width)]
    return parts[0] if len(parts) == 1 else jnp.concatenate(parts, axis=0)


def _ssd_kernel(*refs, cs, nblk, per_seq_state, n_alias):
    (z_ref, xs_ref, bm_ref, cm_ref, dt_ref, cx0_ref, cb0_ref, cc0_ref, h0_ref,
     wx_ref, wb_ref, wc_ref, bx_ref, bb_ref, bc_ref, hp_ref, nrm_ref) = refs[:17]
    (y_ref, cxo_ref, cbo_ref, cco_ref, ho_ref,
     hx_ref, hb_ref, hc_ref, yg_ref, yi_ref, xw_ref, bs_ref, cs_ref, el_ref) = refs[17 + n_alias:]
    rb = ROW_BLOCK
    spb = rb // cs

    @pl.when(pl.program_id(2) == 0)
    def _():
        ho_ref[...] = h0_ref[...]
        cxo_ref[...] = cx0_ref[...]
        cbo_ref[...] = cb0_ref[...]
        cco_ref[...] = cc0_ref[...]

    hx_ref[...] = jnp.zeros_like(hx_ref)
    hb_ref[...] = jnp.zeros_like(hb_ref)
    hc_ref[...] = jnp.zeros_like(hc_ref)

    tri = _chunk_tril(rb, cs)
    tri_b = tri.astype(bf16)
    lane = lax.broadcasted_iota(jnp.int32, (rb, LANES), 1)
    left = lane < SSD_HEAD_DIM
    top = lax.broadcasted_iota(jnp.int32, (LANES, LANES), 0) < SSD_HEAD_DIM
    hp = hp_ref[...]
    a_neg = -jnp.exp(hp[1:2, :])
    npair = SSD_HEADS_PER_GROUP // 2
    nhist = SSD_CONV - 1

    def block_body(blk, carry):
        r0 = pl.multiple_of(blk * rb, rb)
        rs = pl.ds(r0, rb)

        def slot_of(j):
            return blk * spb + j if per_seq_state else 0

        def conv(h_ref, st_ref, u_ref, w_ref, b_ref):
            u = u_ref[rs, :]
            for j in range(spb):
                h_ref[j * cs:j * cs + nhist, :] = st_ref[slot_of(j)]
            hist = h_ref[...]
            pos = lax.broadcasted_iota(jnp.int32, u.shape, 0) & (cs - 1)
            out = b_ref[...] + u * w_ref[nhist:nhist + 1, :]
            for r in range(1, SSD_CONV):
                up = (rb - (nhist - r)) % rb
                older = jnp.where(pos >= r, pltpu.roll(u, r, axis=0), pltpu.roll(hist, up, axis=0) if up else hist)
                out = out + older * w_ref[nhist - r:nhist - r + 1, :]
            for j in range(spb):
                st_ref[slot_of(j)] = u[j * cs + cs - nhist:j * cs + cs, :]
            return _silu(out)

        xc = conv(hx_ref, cxo_ref, xs_ref, wx_ref, bx_ref)
        bmat = conv(hb_ref, cbo_ref, bm_ref, wb_ref, bb_ref)
        cmat = conv(hc_ref, cco_ref, cm_ref, wc_ref, bc_ref)
        bs_ref[...] = bmat
        cs_ref[...] = cmat
        bmb = bmat.astype(bf16)
        cmb = cmat.astype(bf16)

        dt = _softplus(dt_ref[rs, :] + hp[0:1, :])
        acum = _cumsum_rows(tri_b, dt * a_neg)
        acum_t = acum.T
        dt_t = dt.T
        a_last = _rows_of(acum, cs - 1, cs)
        ea = jnp.exp(acum)
        tail = jnp.exp(a_last - acum) * dt
        el_ref[...] = jnp.exp(a_last)
        cb = _dot_nt(cmb, bmb)

        def pair_lanes(x, pp):
            return jnp.where(left, x[:, 2 * pp:2 * pp + 1], x[:, 2 * pp + 1:2 * pp + 2])

        for pp in range(npair):
            ls = slice(pp * LANES, (pp + 1) * LANES)
            xp = xc[:, ls]
            xpb = xp.astype(bf16)

            def head_w(j):
                seg = acum[:, j:j + 1] - acum_t[j:j + 1, :]
                dec = jnp.exp(jnp.where(tri, seg, -jnp.inf))
                return (cb * dec * dt_t[j:j + 1, :]).astype(bf16)

            zero = jnp.zeros_like(xpb)
            y = _dot(head_w(2 * pp), jnp.where(left, xpb, zero)) + _dot(head_w(2 * pp + 1), jnp.where(left, zero, xpb))
            yg_ref[:, ls] = y + xp * pair_lanes(hp[2:3, :], pp)
            xw_ref[:, ls] = xp * pair_lanes(tail, pp)

        def seq_body(j, c):
            slot = slot_of(j)
            rl = pl.ds(pl.multiple_of(j * cs, 8), cs)
            cj = cs_ref[rl, :].astype(bf16)
            bj = bs_ref[rl, :].astype(bf16)
            el = el_ref[pl.ds(j * cs, 1), :]
            for pp in range(npair):
                ls = slice(pp * LANES, (pp + 1) * LANES)
                hstate = ho_ref[slot, ls, :]
                yi_ref[rl, ls] = _dot_nt(cj, hstate.astype(bf16))
                dec_rows = jnp.where(top, el[:, 2 * pp:2 * pp + 1], el[:, 2 * pp + 1:2 * pp + 2])
                ho_ref[slot, ls, :] = hstate * dec_rows + _dot_tn(xw_ref[rl, ls].astype(bf16), bj)
            return c

        lax.fori_loop(0, spb, seq_body, 0, unroll=min(spb, 4))

        for pp in range(npair):
            ls = slice(pp * LANES, (pp + 1) * LANES)
            yg_ref[:, ls] = (yg_ref[:, ls] + yi_ref[:, ls] * pair_lanes(ea, pp)) * _silu(z_ref[rs, ls])
        y_ref[rs, :] = _rms_norm(yg_ref[...], nrm_ref[...]).astype(y_ref.dtype)
        return carry

    lax.fori_loop(0, nblk, block_body, 0, unroll=4 if nblk % 4 == 0 else (2 if nblk % 2 == 0 else 1))


def _ssd(hid, conv_st, h_st, l, st_l, conv_w, conv_b, hp, nrm, prev, y_prev, *, row0, nb, seq_len, cs, nblk):
    rows = nblk * ROW_BLOCK
    per_seq_state = seq_len == cs
    g = rows // seq_len if per_seq_state else 1
    nt = 1 if per_seq_state else seq_len // rows
    rb0 = row0 // rows
    gw = SSD_GROUP_W
    extra = [a for a in (prev, y_prev) if a is not None]

    def rowblk(b, t):
        return rb0 + b * nt + t

    def hid_spec(col0, width):
        return pl.BlockSpec((rows, width), lambda b, gi, t: (rowblk(b, t), col0 // width + gi))

    def st_spec(col0, width):
        return pl.BlockSpec((None, g, 3, width), lambda b, gi, t: (st_l, b, 0, col0 // width + gi))

    def w_spec(nrow, col0, width):
        return pl.BlockSpec((None, nrow, width), lambda b, gi, t: (l, 0, col0 // width + gi))

    def sto_spec(width):
        return pl.BlockSpec((g, 3, width), lambda b, gi, t: (b, 0, gi))

    kern = functools.partial(_ssd_kernel, cs=cs, nblk=nblk, per_seq_state=per_seq_state, n_alias=len(extra))
    args = [hid, hid, hid, hid, hid, conv_st, conv_st, conv_st, h_st,
            conv_w, conv_w, conv_w, conv_b, conv_b, conv_b, hp, nrm] + extra
    aliases = {}
    if prev is not None:
        aliases[17] = 4
    if y_prev is not None:
        aliases[len(args) - 1] = 0
    nbc = SSD_GROUPS * SSD_STATE
    rbk = ROW_BLOCK
    return pl.pallas_call(
        kern,
        out_shape=(jax.ShapeDtypeStruct((hid.shape[0], MIX_W), f32),
                   jax.ShapeDtypeStruct((nb, 3, MIX_W), f32),
                   jax.ShapeDtypeStruct((nb, 3, nbc), f32),
                   jax.ShapeDtypeStruct((nb, 3, nbc), f32),
                   jax.ShapeDtypeStruct((DEPTH, nb, SSD_GROUPS, gw, SSD_STATE), f32)),
        grid=(nb // g, SSD_GROUPS, nt),
        input_output_aliases=aliases,
        in_specs=[
            hid_spec(COL_Z, gw), hid_spec(COL_XS, gw), hid_spec(COL_BM, LANES), hid_spec(COL_CM, LANES),
            hid_spec(COL_DT, LANES),
            st_spec(0, gw), st_spec(MIX_W, LANES), st_spec(MIX_W + nbc, LANES),
            pl.BlockSpec((None, g, None, gw, SSD_STATE), lambda b, gi, t: (st_l, b, gi, 0, 0)),
            w_spec(SSD_CONV, 0, gw), w_spec(SSD_CONV, MIX_W, LANES), w_spec(SSD_CONV, MIX_W + nbc, LANES),
            w_spec(1, 0, gw), w_spec(1, MIX_W, LANES), w_spec(1, MIX_W + nbc, LANES),
            pl.BlockSpec((None, None, 8, LANES), lambda b, gi, t: (l, gi, 0, 0)),
            pl.BlockSpec((None, 1, gw), lambda b, gi, t: (l, 0, gi)),
        ] + [pl.BlockSpec(memory_space=pl.ANY)] * len(extra),
        out_specs=(
            pl.BlockSpec((rows, gw), lambda b, gi, t: (rowblk(b, t), gi)),
            sto_spec(gw), sto_spec(LANES), sto_spec(LANES),
            pl.BlockSpec((None, g, None, gw, SSD_STATE), lambda b, gi, t: (l, b, gi, 0, 0)),
        ),
        scratch_shapes=[pltpu.VMEM((rbk, gw), f32), pltpu.VMEM((rbk, LANES), f32), pltpu.VMEM((rbk, LANES), f32),
                        pltpu.VMEM((rbk, gw), f32), pltpu.VMEM((rbk, gw), f32), pltpu.VMEM((rbk, gw), f32),
                        pltpu.VMEM((rbk, LANES), f32), pltpu.VMEM((rbk, LANES), f32), pltpu.VMEM((rbk, LANES), f32)],
        compiler_params=_cp("parallel", "parallel", "arbitrary"),
        name="ssd_scan",
    )(*args)


def _mlstm_kernel(*refs, cs, nblk, per_seq_state, n_alias):
    q_ref, k_ref, v_ref, o_ref, gt_ref, c0_ref, n0_ref, m0_ref, gb_ref, nrm_ref = refs[:10]
    y_ref, co_ref, no_ref, mo_ref, num_ref, kw_ref, dec_ref, mt_ref, sp_ref = refs[10 + n_alias:]
    rb = ROW_BLOCK
    spb = rb // cs

    @pl.when(pl.program_id(2) == 0)
    def _():
        co_ref[...] = c0_ref[...]
        no_ref[...] = n0_ref[...]
        mo_ref[...] = m0_ref[...]

    tri = _chunk_tril(rb, cs)
    tri_b = tri.astype(bf16)
    kscale = ML_DK ** -0.5

    def block_body(blk, carry):
        r0 = pl.multiple_of(blk * rb, rb)
        rs = pl.ds(r0, rb)

        def slot_of(j):
            return blk * spb + j if per_seq_state else 0

        def per_seq(ref, width):
            parts = [jnp.broadcast_to(ref[slot_of(j)], (cs, width)) for j in range(spb)]
            return parts[0] if spb == 1 else jnp.concatenate(parts, axis=0)

        gates = gt_ref[rs, :] + gb_ref[...]
        bfull = _cumsum_rows(tri_b, _log_sigmoid(gates))
        logi_c = gates[:, 0:1]
        b_c = bfull[:, 1:2]
        logi_r = gates.T[0:1, :]
        b_r = bfull.T[1:2, :]
        m_prev = per_seq(mo_ref, 1)
        n_prev = per_seq(no_ref, ML_DK)

        dmat = jnp.where(tri, b_c - b_r + logi_r, -jnp.inf)
        prev = b_c + m_prev
        mt = jnp.maximum(prev, jnp.max(dmat, axis=-1, keepdims=True))
        wts = jnp.exp(dmat - mt)
        sprev = jnp.exp(prev - mt)

        qh = q_ref[rs, :]
        kh = k_ref[rs, :] * kscale
        qk = _dot_nt(qh.astype(bf16), kh.astype(bf16)) * wts
        num_ref[...] = _dot(qk.astype(bf16), v_ref[rs, :].astype(bf16))
        den = jnp.sum(qk, axis=-1, keepdims=True) + jnp.sum(qh * n_prev, axis=-1, keepdims=True) * sprev

        m_new = _rows_of(mt, cs - 1, cs)
        b_last = _rows_of(b_c, cs - 1, cs)
        kw_ref[...] = kh * jnp.exp(b_last - b_c + logi_c - m_new)
        dec_ref[...] = jnp.broadcast_to(jnp.exp(b_last + m_prev - m_new), (rb, LANES))
        mt_ref[...] = jnp.broadcast_to(m_new, (rb, LANES))
        sp_ref[...] = jnp.broadcast_to(sprev, (rb, LANES))

        def seq_body(j, c):
            slot = slot_of(j)
            rl = pl.ds(pl.multiple_of(j * cs, 8), cs)
            rg = pl.ds(pl.multiple_of(r0 + j * cs, 8), cs)
            cst = co_ref[slot]
            num_ref[rl, :] += _dot(q_ref[rg, :].astype(bf16), cst.astype(bf16)) * sp_ref[rl, 0:1]
            kw = kw_ref[rl, :]
            dec = dec_ref[pl.ds(j * cs, 1), 0:1]
            co_ref[slot] = cst * dec + _dot_tn(kw.astype(bf16), v_ref[rg, :].astype(bf16))
            no_ref[slot] = no_ref[slot] * dec + jnp.sum(kw, axis=0, keepdims=True)
            mo_ref[slot] = mt_ref[pl.ds(j * cs, 1), 0:1]
            return c

        lax.fori_loop(0, spb, seq_body, 0, unroll=min(spb, 4))

        hh = num_ref[...] / jnp.maximum(jnp.abs(den), jnp.exp(-mt))
        y = _rms_norm(hh, nrm_ref[...]) * _sigmoid(o_ref[rs, :])
        y_ref[rs, :] = y.astype(y_ref.dtype)
        return carry

    lax.fori_loop(0, nblk, block_body, 0, unroll=4 if nblk % 4 == 0 else (2 if nblk % 2 == 0 else 1))


def _mlstm(hid, c_st, n_st, m_st, l, st_l, gbias, nrm, prev, y_prev, *, row0, nb, seq_len, cs, nblk):
    rows = nblk * ROW_BLOCK
    per_seq_state = seq_len == cs
    g = rows // seq_len if per_seq_state else 1
    nt = 1 if per_seq_state else seq_len // rows
    rb0 = row0 // rows
    extra = [a for a in (prev, y_prev) if a is not None]

    def hid_spec(col0, width):
        return pl.BlockSpec((rows, width), lambda b, h, t: (rb0 + b * nt + t, col0 // width + h))

    kern = functools.partial(_mlstm_kernel, cs=cs, nblk=nblk, per_seq_state=per_seq_state, n_alias=len(extra))
    args = [hid, hid, hid, hid, hid, c_st, n_st, m_st, gbias, nrm] + extra
    aliases = {}
    if prev is not None:
        aliases[10] = 1
    if y_prev is not None:
        aliases[len(args) - 1] = 0
    rbk = ROW_BLOCK
    return pl.pallas_call(
        kern,
        input_output_aliases=aliases,
        out_shape=(jax.ShapeDtypeStruct((hid.shape[0], MIX_W), f32),
                   jax.ShapeDtypeStruct((DEPTH, nb, ML_HEADS, ML_DK, ML_DV), f32),
                   jax.ShapeDtypeStruct((nb, ML_HEADS, 1, ML_DK), f32),
                   jax.ShapeDtypeStruct((nb, ML_HEADS, 1, 1), f32)),
        grid=(nb // g, ML_HEADS, nt),
        in_specs=[
            hid_spec(COL_MQ, ML_DK), hid_spec(COL_MK, ML_DK), hid_spec(COL_MV, ML_DV), hid_spec(COL_MO, ML_DV),
            hid_spec(COL_MG, LANES),
            pl.BlockSpec((None, g, None, ML_DK, ML_DV), lambda b, h, t: (st_l, b, h, 0, 0)),
            pl.BlockSpec((None, g, None, 1, ML_DK), lambda b, h, t: (st_l, b, h, 0, 0)),
            pl.BlockSpec((None, g, None, 1, 1), lambda b, h, t: (st_l, b, h, 0, 0)),
            pl.BlockSpec((None, None, 1, LANES), lambda b, h, t: (l, h, 0, 0)),
            pl.BlockSpec((None, 1, ML_DV), lambda b, h, t: (l, 0, h)),
        ] + [pl.BlockSpec(memory_space=pl.ANY)] * len(extra),
        out_specs=(
            pl.BlockSpec((rows, ML_DV), lambda b, h, t: (rb0 + b * nt + t, h)),
            pl.BlockSpec((None, g, None, ML_DK, ML_DV), lambda b, h, t: (l, b, h, 0, 0)),
            pl.BlockSpec((g, None, 1, ML_DK), lambda b, h, t: (b, h, 0, 0)),
            pl.BlockSpec((g, None, 1, 1), lambda b, h, t: (b, h, 0, 0)),
        ),
        scratch_shapes=[pltpu.VMEM((rbk, ML_DV), f32), pltpu.VMEM((rbk, ML_DK), f32), pltpu.VMEM((rbk, LANES), f32),
                        pltpu.VMEM((rbk, LANES), f32), pltpu.VMEM((rbk, LANES), f32)],
        compiler_params=_cp("parallel", "parallel", "arbitrary"),
        name="mlstm_scan",
    )(*args)


HG_ROWS = ROW_BLOCK
HG_TILE = 8


def _hgrn_kernel(*refs, cs, nblk, per_seq_state, layer, n_alias):
    q_ref, f_ref, i_ref, g_ref, s0_ref, lg_ref, nrm_ref = refs[:7]
    y_ref, so_ref, b_ref, qe_ref, kd_ref, v_ref, yi_ref = refs[7 + n_alias:]
    rb = HG_ROWS
    spb = rb // cs

    @pl.when(pl.program_id(2) == 0)
    def _():
        so_ref[...] = s0_ref[...]

    lg = lg_ref[...]
    pe = jnp.exp(lg - jnp.max(lg, axis=0, keepdims=True))
    pw = pe / jnp.sum(pe, axis=0, keepdims=True)
    lb = jnp.zeros((1, HG_DK), f32)
    for j in range(1, layer + 1):
        lb = lb + pw[j:j + 1, :]
    log_lb = jnp.log(lb)
    log_1m = jnp.log1p(-lb)

    row_a = lax.broadcasted_iota(jnp.int32, (rb, rb), 0)
    lane_a = lax.broadcasted_iota(jnp.int32, (rb, rb), 1)
    row_c = lax.broadcasted_iota(jnp.int32, (rb, HG_DK), 0)
    tri_b = ((row_a >= lane_a) & ((row_a ^ lane_a) < cs)).astype(bf16)

    def block_body(blk, carry):
        rs = pl.ds(pl.multiple_of(blk * rb, rb), rb)
        fr = f_ref[rs, :]
        en = jnp.exp(-jnp.abs(fr))
        log_sig = jnp.minimum(fr, 0.0) - jnp.log1p(en)
        sig_neg = jnp.where(fr >= 0.0, en, 1.0) / (1.0 + en)
        if layer == 0:
            logf, kk = log_sig, sig_neg
        else:
            u = log_1m + log_sig
            mx = jnp.maximum(log_lb, u)
            mn = jnp.minimum(log_lb, u)
            logf = mx + jnp.log1p(jnp.exp(mn - mx))
            kk = (1.0 - lb) * sig_neg
        qq = _silu(q_ref[rs, :])
        v = i_ref[rs, :]
        b = _cumsum_rows(tri_b, logf)

        att = jnp.where(lane_a == row_a, jnp.sum(qq * kk, axis=-1, keepdims=True), 0.0)
        for r in range(1, HG_TILE):
            valid = (row_c & (HG_TILE - 1)) >= r
            e = jnp.exp(b - pltpu.roll(b, r, axis=0))
            col = jnp.sum(jnp.where(valid, qq * e * pltpu.roll(kk, r, axis=0), 0.0), axis=-1, keepdims=True)
            att = jnp.where(lane_a == row_a - r, col, att)
        size = cs
        while size > HG_TILE:
            half = size // 2
            e = jnp.exp(-jnp.abs(b - _rows_of(b, half - 1, size)))
            p = _dot_nt((qq * e).astype(bf16), (kk * e).astype(bf16))
            split = (row_a ^ lane_a)
            own = (split >= half) & (split < size) & ((row_a & half) != 0)
            att = jnp.where(own, p, att)
            size = half

        yi_ref[...] = _dot(att.astype(bf16), v.astype(bf16))
        b_ref[...] = b
        qe_ref[...] = qq * jnp.exp(b)
        kd_ref[...] = kk * jnp.exp(_rows_of(b, cs - 1, cs) - b)
        v_ref[...] = v

        def seq_body(j, c):
            slot = blk * spb + j if per_seq_state else 0
            r = pl.ds(pl.multiple_of(j * cs, 8), cs)
            st = so_ref[slot]
            yi_ref[r, :] += _dot(qe_ref[r, :].astype(bf16), st.astype(bf16))
            b_last = b_ref[pl.ds(j * cs + cs - 1, 1), :]
            dcol = jnp.exp(jnp.broadcast_to(b_last, (8, HG_DK))).T[:, 0:1]
            so_ref[slot] = st * dcol + _dot_tn(kd_ref[r, :].astype(bf16), v_ref[r, :].astype(bf16))
            return c

        lax.fori_loop(0, spb, seq_body, 0, unroll=min(spb, 4))
        y = _rms_norm(yi_ref[...], nrm_ref[...]) * _sigmoid(g_ref[rs, :])
        y_ref[rs, :] = y.astype(y_ref.dtype)
        return carry

    lax.fori_loop(0, nblk, block_body, 0, unroll=4 if nblk % 4 == 0 else (2 if nblk % 2 == 0 else 1))


def _hgrn(hid, s_st, l, st_l, logits, nrm, prev, y_prev, *, row0, nb, seq_len, cs, nblk):
    rows = nblk * HG_ROWS
    per_seq_state = seq_len == cs
    g = rows // seq_len if per_seq_state else 1
    nt = 1 if per_seq_state else seq_len // rows
    rb0 = row0 // rows
    extra = [a for a in (prev, y_prev) if a is not None]

    def hid_spec(col0):
        return pl.BlockSpec((rows, LANES), lambda b, h, t: (rb0 + b * nt + t, col0 // LANES + h))

    kern = functools.partial(_hgrn_kernel, cs=cs, nblk=nblk, per_seq_state=per_seq_state, layer=l, n_alias=len(extra))
    in_specs = [
        hid_spec(COL_HQ), hid_spec(COL_HF), hid_spec(COL_HI), hid_spec(COL_HG),
        pl.BlockSpec((None, g, None, HG_DK, HG_DV), lambda b, h, t: (st_l, b, h, 0, 0)),
        pl.BlockSpec((DEPTH, LANES), lambda b, h, t: (0, h)),
        pl.BlockSpec((None, 1, LANES), lambda b, h, t: (l, 0, h)),
    ]
    args = [hid, hid, hid, hid, s_st, logits, nrm] + extra
    in_specs += [pl.BlockSpec(memory_space=pl.ANY)] * len(extra)
    aliases = {}
    if prev is not None:
        aliases[7] = 1
    if y_prev is not None:
        aliases[len(args) - 1] = 0
    return pl.pallas_call(
        kern,
        out_shape=(jax.ShapeDtypeStruct((hid.shape[0], MIX_W), f32),
                   jax.ShapeDtypeStruct((DEPTH, nb, HG_HEADS, HG_DK, HG_DV), f32)),
        grid=(nb // g, HG_HEADS, nt),
        in_specs=in_specs,
        out_specs=(
            pl.BlockSpec((rows, LANES), lambda b, h, t: (rb0 + b * nt + t, h)),
            pl.BlockSpec((None, g, None, HG_DK, HG_DV), lambda b, h, t: (l, b, h, 0, 0)),
        ),
        scratch_shapes=[pltpu.VMEM((HG_ROWS, HG_DK), f32)] * 5,
        input_output_aliases=aliases,
        compiler_params=_cp("parallel", "parallel", "arbitrary"),
        name="hgrn_scan",
    )(*args)


def _prep_w_in(w_in):
    off = {}
    start = 0
    names = ("z", "xbc", "dt", "mq", "mk", "mv", "mo", "mi", "mf", "hq", "hf", "hi", "hg", "gate")
    sizes = (MIX_W, MIX_W + 2 * SSD_GROUPS * SSD_STATE, SSD_HEADS, ML_HEADS * ML_DK, ML_HEADS * ML_DK, MIX_W, MIX_W,
             ML_HEADS, ML_HEADS, MIX_W, MIX_W, MIX_W, MIX_W, 3 * D_MODEL)
    for nme, sz in zip(names, sizes):
        off[nme] = (start, sz)
        start += sz

    def seg(nme, lo=0, hi=None):
        s, sz = off[nme]
        hi = sz if hi is None else hi
        return w_in[:, :, s + lo:s + hi]

    nbc = SSD_GROUPS * SSD_STATE
    dpth, kdim = w_in.shape[0], w_in.shape[1]
    dt = seg("dt").reshape(dpth, kdim, SSD_GROUPS, SSD_HEADS_PER_GROUP)
    dt = jnp.pad(dt, ((0, 0), (0, 0), (0, 0), (0, LANES - SSD_HEADS_PER_GROUP))).reshape(dpth, kdim, SSD_GROUPS * LANES)
    mg = jnp.stack([seg("mi"), seg("mf")], axis=-1)
    mg = jnp.pad(mg, ((0, 0), (0, 0), (0, 0), (0, LANES - 2))).reshape(dpth, kdim, ML_HEADS * LANES)
    cols = [seg("z"), seg("xbc", 0, MIX_W), seg("mv"), seg("mo"), seg("hq"), seg("hf"), seg("hi"), seg("hg"),
            seg("gate"), seg("mq"), seg("mk"), seg("xbc", MIX_W, MIX_W + nbc), seg("xbc", MIX_W + nbc, MIX_W + 2 * nbc),
            dt, mg]
    return jnp.concatenate([c.astype(bf16) for c in cols], axis=-1)


def _row_major_groups(x_prompt, x_sample):
    return jnp.concatenate([x_prompt.reshape(-1, D_MODEL), x_sample.reshape(-1, D_MODEL)], axis=0)


def kernel(x_prompt, x_sample, mem_prompt, state_ssd_conv, state_ssd, state_mlstm_c, state_mlstm_n, state_mlstm_m, state_hgrn, cache_mem_k, cache_mem_v, ln_g, ln_b, ffn_w1, ffn_w3, ffn_w2, w_in, ssd_conv_w, ssd_conv_b, ssd_dt_bias, ssd_a_log, ssd_d, ssd_norm, ml_gate_bias, ml_norm, hg_lb_logits, hg_norm, w_branch, w_mix_out, x_wq, x_wk, x_wv, x_wo):
    nbp, lp = x_prompt.shape[0], x_prompt.shape[1]
    nbs, ls = x_sample.shape[0], x_sample.shape[1]
    mp = nbp * lp
    depth = w_in.shape[0]

    w1b = ffn_w1.astype(bf16)
    w3b = ffn_w3.astype(bf16)
    w2b = ffn_w2.astype(bf16)
    w_hid = _prep_w_in(w_in)
    wb = w_branch.astype(bf16)
    w_mo = w_mix_out.astype(bf16)
    w_q = x_wq.astype(bf16)
    w_o = x_wo.astype(bf16)
    w_kv = jnp.concatenate([x_wk, x_wv], axis=-1).astype(bf16)
    lng = ln_g.reshape(depth, 4, 1, D_MODEL)
    lnb = ln_b.reshape(depth, 4, 1, D_MODEL)

    def per_head_rows(*rows):
        r = jnp.stack([a.reshape(depth, SSD_GROUPS, SSD_HEADS_PER_GROUP) for a in rows], axis=2)
        return jnp.pad(r, ((0, 0), (0, 0), (0, 8 - len(rows)), (0, LANES - SSD_HEADS_PER_GROUP)))

    ssd_hp = per_head_rows(ssd_dt_bias, ssd_a_log, ssd_d)
    ssd_nrm = ssd_norm.reshape(depth, 1, MIX_W)
    conv_b = ssd_conv_b.reshape(depth, 1, -1)
    ml_gb = jnp.pad(jnp.swapaxes(ml_gate_bias, 1, 2), ((0, 0), (0, 0), (0, LANES - 2))).reshape(depth, ML_HEADS, 1, LANES)
    ml_nrm = ml_norm.reshape(depth, 1, MIX_W)
    hg_nrm = hg_norm.reshape(depth, 1, MIX_W)

    s_ssd = state_ssd.reshape(depth, nbs, SSD_GROUPS, SSD_GROUP_W, SSD_STATE)
    s_n = state_mlstm_n.reshape(depth, nbs, ML_HEADS, 1, ML_DK)
    s_m = state_mlstm_m.reshape(depth, nbs, ML_HEADS, 1, 1)
    nct = X_HEAD_DIM // LANES

    def cache_rows(c):
        c = c.reshape(depth, nbs, MEM_LEN, X_HEADS, nct, LANES)
        return jnp.swapaxes(c, 3, 4).reshape(depth, nbs, MEM_LEN * nct * X_HEADS, LANES)

    cache_k = cache_rows(cache_mem_k)
    cache_v = cache_rows(cache_mem_v)
    z_conv = jnp.zeros((1, nbp, SSD_CONV - 1, state_ssd_conv.shape[-1]), f32)
    z_ssd = jnp.zeros((1, nbp, SSD_GROUPS, SSD_GROUP_W, SSD_STATE), f32)
    z_c = jnp.zeros((1, nbp, ML_HEADS, ML_DK, ML_DV), f32)
    z_n = jnp.zeros((1, nbp, ML_HEADS, 1, ML_DK), f32)
    z_m = jnp.zeros((1, nbp, ML_HEADS, 1, 1), f32)
    z_hg = jnp.zeros((1, nbp, HG_HEADS, HG_DK, HG_DV), f32)

    mem_rows = mem_prompt.reshape(nbp * MEM_LEN, D_MODEL)
    x = _row_major_groups(x_prompt, x_sample)

    p_small = [[] for _ in range(5)]
    s_small = [[] for _ in range(3)]
    p_ssd = p_c = p_hg = s_ssd_o = s_c_o = s_hg_o = None
    pc, sc = 128, ls
    blocks_per_seq = lp // ROW_BLOCK
    pb = min(8, blocks_per_seq)
    pb_hg = min(16, blocks_per_seq)
    sb_hg = min(4, nbs * ls // ROW_BLOCK)

    for l in range(depth):
        x = _ffn_ln(x, w1b, w3b, w2b, lng, lnb, l, 0, 0)
        hid = _matmul(x, w_hid, l, f32, tm=1024, tn=1024, name="in_proj")

        yp, cx, cb_, cc, p_ssd = _ssd(hid, z_conv, z_ssd, l, 0, ssd_conv_w, conv_b, ssd_hp, ssd_nrm, p_ssd, None,
                                      row0=0, nb=nbp, seq_len=lp, cs=pc, nblk=pb)
        y_ssd, sx, sb, scc, s_ssd_o = _ssd(hid, state_ssd_conv, s_ssd, l, l, ssd_conv_w, conv_b, ssd_hp, ssd_nrm, s_ssd_o,
                                           yp, row0=mp, nb=nbs, seq_len=ls, cs=sc, nblk=1)
        p_small[0].append(jnp.concatenate([cx, cb_, cc], axis=-1))
        s_small[0].append(jnp.concatenate([sx, sb, scc], axis=-1))

        yp, p_c, n1, m1 = _mlstm(hid, z_c, z_n, z_m, l, 0, ml_gb, ml_nrm, p_c, None,
                                 row0=0, nb=nbp, seq_len=lp, cs=pc, nblk=pb)
        y_ml, s_c_o, sn1, sm1 = _mlstm(hid, state_mlstm_c, s_n, s_m, l, l, ml_gb, ml_nrm, s_c_o, yp,
                                       row0=mp, nb=nbs, seq_len=ls, cs=sc, nblk=1)
        p_small[1].append(n1.reshape(nbp, ML_HEADS, ML_DK))
        p_small[2].append(m1.reshape(nbp, ML_HEADS))
        s_small[1].append(sn1.reshape(nbs, ML_HEADS, ML_DK))
        s_small[2].append(sm1.reshape(nbs, ML_HEADS))

        yp, p_hg = _hgrn(hid, z_hg, l, 0, hg_lb_logits, hg_nrm, p_hg, None,
                         row0=0, nb=nbp, seq_len=lp, cs=pc, nblk=pb_hg)
        y_hg, s_hg_o = _hgrn(hid, state_hgrn, l, l, hg_lb_logits, hg_nrm, s_hg_o, yp,
                             row0=mp, nb=nbs, seq_len=ls, cs=sc, nblk=sb_hg)

        s_mix = _merge((y_ssd, y_ml, y_hg), hid, wb, l)
        x = _res_ln(x, s_mix, w_mo, lng, lnb, l, 1)

        kv = _matmul(mem_rows, w_kv, l, f32, tm=nbp * MEM_LEN, name="mem_kv")
        pk = kv[:, :D_MODEL].reshape(nbp, MEM_LEN, D_MODEL)
        pv = kv[:, D_MODEL:].reshape(nbp, MEM_LEN, D_MODEL)
        p_small[3].append(pk.reshape(nbp, MEM_LEN, X_HEADS, X_HEAD_DIM))
        p_small[4].append(pv.reshape(nbp, MEM_LEN, X_HEADS, X_HEAD_DIM))

        q = _matmul(x, w_q, l, f32, tm=1024, name="q_proj")
        op = _attention(q, pk, pv, l, 0, nbp, lp, nseq=1, rows=512)
        o = _attention(q, cache_k, cache_v, l, mp, nbs, ls, nseq=4, rows=ls, o_prev=op)
        x = _res_ln(x, o, w_o, lng, lnb, l, 2)
        x = _ffn_ln(x, w1b, w3b, w2b, lng, lnb, l, 1, 3)

    y_prompt = x[:mp].reshape(nbp, lp, D_MODEL)
    y_sample = x[mp:].reshape(nbs, ls, D_MODEL)
    st = jnp.stack
    return (y_prompt, y_sample,
            st(p_small[0]), p_ssd.reshape(depth, nbp, SSD_HEADS, SSD_HEAD_DIM, SSD_STATE), p_c, st(p_small[1]),
            st(p_small[2]), p_hg, st(p_small[3]), st(p_small[4]),
            st(s_small[0]), s_ssd_o.reshape(depth, nbs, SSD_HEADS, SSD_HEAD_DIM, SSD_STATE), s_c_o, st(s_small[1]),
            st(s_small[2]), s_hg_o)
```
